```python
import math
import jax, jax.numpy as jnp
from jax import lax
import numpy as np

D_MODEL = 4096
BATCH = 2
SEQ = 8192
DEPTH = 1

A_WIDTH = D_MODEL // 2
A_CHUNK = 128
A_HEADS = 16
A_HEAD_DIM = A_WIDTH // A_HEADS
B_WIDTH = D_MODEL // 4
S5_GROUP_DIM = 16
S5_GROUPS = B_WIDTH // S5_GROUP_DIM
S5_STATE = 64
DT_MIN = 1e-3
DT_MAX = 1e-1
D_FF = 4 * D_MODEL
NORM_EPS = 1e-6
IN_WIDTH = 2 * A_WIDTH + B_WIDTH + 2 * D_MODEL

kernel_name = "hybrid_gmlp_s5_gated_block"


def rms_norm(x, g):
    xf = x.astype(jnp.float32)
    y = xf * lax.rsqrt(jnp.mean(xf * xf, axis=-1, keepdims=True) + NORM_EPS)
    return (y * g.astype(jnp.float32)).astype(x.dtype)


def layer_norm(x, g, b):
    xf = x.astype(jnp.float32)
    mu = jnp.mean(xf, axis=-1, keepdims=True)
    xc = xf - mu
    y = xc * lax.rsqrt(jnp.mean(xc * xc, axis=-1, keepdims=True) + NORM_EPS)
    return (y * g.astype(jnp.float32) + b.astype(jnp.float32)).astype(x.dtype)


def chunked_spatial_gating(u, v, w_s, b_s):
    bt, seq_len, _ = u.shape
    n_chunks = seq_len // A_CHUNK
    v = v.reshape(bt, n_chunks, A_CHUNK, A_HEADS, A_HEAD_DIM)
    causal = jnp.tril(jnp.ones((A_CHUNK, A_CHUNK), dtype=bool))
    w = jnp.where(causal[None], w_s, jnp.zeros_like(w_s))
    mixed = jnp.einsum('hts,bcshe->bcthe', w, v) + b_s.T[None, None, :, :, None]
    return u * mixed.reshape(bt, seq_len, A_WIDTH)


def s5_layer(u, lam_re, lam_im, log_dt, b_re, b_im, c_re, c_im, d_skip):
    f32 = jnp.float32
    bt, seq_len, _ = u.shape
    uf = u.astype(f32).reshape(bt, seq_len, S5_GROUPS, S5_GROUP_DIM)
    lam = lax.complex(lam_re.astype(f32), lam_im.astype(f32))
    dt = jnp.exp(log_dt.astype(f32))[:, None]
    a_bar = jnp.exp(lam * dt)
    b_mat = lax.complex(b_re.astype(f32), b_im.astype(f32))
    b_bar = ((a_bar - 1.0) / lam)[..., None] * b_mat
    c_mat = lax.complex(c_re.astype(f32), c_im.astype(f32))
    bu = jnp.einsum('gph,blgh->blgp', b_bar, uf.astype(jnp.complex64))
    a_seq = jnp.broadcast_to(a_bar, bu.shape)

    def combine(left, right):
        a_l, b_l = left
        a_r, b_r = right
        return a_r * a_l, a_r * b_l + b_r

    _, states = lax.associative_scan(combine, (a_seq, bu), axis=1)
    y = jnp.einsum('ghp,blgp->blgh', c_mat, states).real
    y = y + d_skip.astype(f32).reshape(S5_GROUPS, S5_GROUP_DIM) * uf
    return y.reshape(bt, seq_len, B_WIDTH).astype(u.dtype)


def setup_inputs(seed: int = 0) -> dict:
    key = jax.random.key(seed)
    ks = jax.random.split(key, 32)
    f32 = jnp.float32
    L = DEPTH

    def nrm(k, shape, scale):
        return jax.random.normal(k, shape, f32) * scale

    def gain(k, shape):
        return 1.0 + 0.02 * jax.random.normal(k, shape, f32)

    x = jax.random.normal(ks[0], (BATCH, SEQ, D_MODEL), f32)
    norm_mix_pre = gain(ks[1], (L, D_MODEL))
    w_in = nrm(ks[2], (L, D_MODEL, IN_WIDTH), D_MODEL ** -0.5)
    v_norm_g = gain(ks[3], (L, A_WIDTH))
    v_norm_b = nrm(ks[4], (L, A_WIDTH), 0.02)
    w_spatial = nrm(ks[5], (L, A_HEADS, A_CHUNK, A_CHUNK), 0.5 * A_CHUNK ** -0.5)
    b_spatial = gain(ks[6], (L, A_HEADS, A_CHUNK))
    w_proj_a = nrm(ks[7], (L, A_WIDTH, D_MODEL), A_WIDTH ** -0.5)
    n = jnp.arange(S5_STATE, dtype=f32)
    lam_re = -0.5 + 0.01 * jax.random.normal(ks[8], (L, S5_GROUPS, S5_STATE), f32)
    lam_im = math.pi * n + 0.01 * jax.random.normal(ks[9], (L, S5_GROUPS, S5_STATE), f32)
    log_dt = jax.random.uniform(ks[10], (L, S5_GROUPS), f32, math.log(DT_MIN), math.log(DT_MAX))
    b_re = nrm(ks[11], (L, S5_GROUPS, S5_STATE, S5_GROUP_DIM), (2 * S5_GROUP_DIM) ** -0.5)
    b_im = nrm(ks[12], (L, S5_GROUPS, S5_STATE, S5_GROUP_DIM), (2 * S5_GROUP_DIM) ** -0.5)
    c_re = nrm(ks[13], (L, S5_GROUPS, S5_GROUP_DIM, S5_STATE), (2 * S5_STATE) ** -0.5)
    c_im = nrm(ks[14], (L, S5_GROUPS, S5_GROUP_DIM, S5_STATE), (2 * S5_STATE) ** -0.5)
    d_skip = nrm(ks[15], (L, B_WIDTH), 1.0)
    w_glu_a = nrm(ks[16], (L, B_WIDTH, D_MODEL), B_WIDTH ** -0.5)
    w_glu_b = nrm(ks[17], (L, B_WIDTH, D_MODEL), B_WIDTH ** -0.5)
    w_out = nrm(ks[18], (L, D_MODEL, D_MODEL), D_MODEL ** -0.5)
    norm_mix_post = gain(ks[19], (L, D_MODEL))
    norm_mlp_pre = gain(ks[20], (L, D_MODEL))
    w_ff_up = nrm(ks[21], (L, D_MODEL, D_FF), D_MODEL ** -0.5)
    w_ff_down = nrm(ks[22], (L, D_FF, D_MODEL), D_FF ** -0.5)
    norm_mlp_post = gain(ks[23], (L, D_MODEL))
    return {"x": x, "norm_mix_pre": norm_mix_pre, "w_in": w_in, "v_norm_g": v_norm_g,
            "v_norm_b": v_norm_b, "w_spatial": w_spatial, "b_spatial": b_spatial,
            "w_proj_a": w_proj_a, "lam_re": lam_re, "lam_im": lam_im, "log_dt": log_dt,
            "b_re": b_re, "b_im": b_im, "c_re": c_re, "c_im": c_im, "d_skip": d_skip,
            "w_glu_a": w_glu_a, "w_glu_b": w_glu_b, "w_out": w_out,
            "norm_mix_post": norm_mix_post, "norm_mlp_pre": norm_mlp_pre,
            "w_ff_up": w_ff_up, "w_ff_down": w_ff_down, "norm_mlp_post": norm_mlp_post}


def reference(x, norm_mix_pre, w_in, v_norm_g, v_norm_b, w_spatial, b_spatial, w_proj_a,
              lam_re, lam_im, log_dt, b_re, b_im, c_re, c_im, d_skip, w_glu_a, w_glu_b,
              w_out, norm_mix_post, norm_mlp_pre, w_ff_up, w_ff_down, norm_mlp_post):
    for l in range(DEPTH):
        h = rms_norm(x, norm_mix_pre[l])
        proj = h @ w_in[l]
        o1 = 2 * A_WIDTH
        o2 = o1 + B_WIDTH
        o3 = o2 + D_MODEL
        z_a = jax.nn.gelu(proj[..., :o1])
        u_a, v_a = z_a[..., :A_WIDTH], z_a[..., A_WIDTH:]
        x_b = proj[..., o1:o2]
        gate_a = jax.nn.sigmoid(proj[..., o2:o3])
        gate_b = jax.nn.sigmoid(proj[..., o3:])
        v_a = layer_norm(v_a, v_norm_g[l], v_norm_b[l])
        s_a = chunked_spatial_gating(u_a, v_a, w_spatial[l], b_spatial[l])
        branch_a = s_a @ w_proj_a[l]
        y_b = s5_layer(x_b, lam_re[l], lam_im[l], log_dt[l], b_re[l], b_im[l],
                       c_re[l], c_im[l], d_skip[l])
        z_b = jax.nn.gelu(y_b)
        branch_b = (z_b @ w_glu_a[l]) * jax.nn.sigmoid(z_b @ w_glu_b[l])
        mix = (gate_a * branch_a + gate_b * branch_b) @ w_out[l]
        x = x + rms_norm(mix, norm_mix_post[l])
        h = rms_norm(x, norm_mlp_pre[l])
        ff = jnp.square(jax.nn.relu(h @ w_ff_up[l])) @ w_ff_down[l]
        x = x + rms_norm(ff, norm_mlp_post[l])
    return x
```

```python
import functools

import jax
import jax.numpy as jnp
from jax import lax
from jax.experimental import pallas as pl
from jax.experimental.pallas import tpu as pltpu

F32 = jnp.float32
BF16 = jnp.bfloat16
NORM_EPS = 1e-6
HIGHEST = lax.Precision.HIGHEST

A_CHUNK = 128
S5_CHUNK = 16
S5_PAIR = 2
S5_ROWS = 8
MIB = 1024 * 1024


def _rms_scale(x, gain):
    ms = jnp.mean(x * x, axis=-1, keepdims=True)
    return x * lax.rsqrt(ms + NORM_EPS) * gain


def _in_proj_kernel(x_ref, g_ref, w_ref, o_ref, h_sc, *, gelu_tiles):
    j = pl.program_id(1)

    @pl.when(j == 0)
    def _():
        h_sc[...] = _rms_scale(x_ref[...], g_ref[...]).astype(BF16)

    acc = jnp.dot(h_sc[...], w_ref[...], preferred_element_type=F32)

    @pl.when(j < gelu_tiles)
    def _():
        o_ref[...] = jax.nn.gelu(acc).astype(BF16)

    @pl.when(j == gelu_tiles)
    def _():
        o_ref[...] = acc.astype(BF16)

    @pl.when(j > gelu_tiles)
    def _():
        o_ref[...] = jax.nn.sigmoid(acc).astype(BF16)


def _in_proj(x2, gain, w_bf16, *, gelu_width, lin_width, tm=512, tn=1024):
    t, d = x2.shape
    n = w_bf16.shape[1]
    assert gelu_width % tn == 0 and lin_width == tn and t % tm == 0 and n % tn == 0
    vmem = 2 * tm * d * 4 + tm * d * 2 + 2 * d * tn * 2 + 2 * tm * tn * 2 + 8 * tm * tn * 4
    return pl.pallas_call(
        functools.partial(_in_proj_kernel, gelu_tiles=gelu_width // tn),
        out_shape=jax.ShapeDtypeStruct((t, n), BF16),
        grid=(t // tm, n // tn),
        in_specs=[
            pl.BlockSpec((tm, d), lambda i, j: (i, 0)),
            pl.BlockSpec((1, d), lambda i, j: (0, 0)),
            pl.BlockSpec((d, tn), lambda i, j: (0, j)),
        ],
        out_specs=pl.BlockSpec((tm, tn), lambda i, j: (i, j)),
        scratch_shapes=[pltpu.VMEM((tm, d), BF16)],
        compiler_params=pltpu.CompilerParams(
            dimension_semantics=("arbitrary", "arbitrary"), vmem_limit_bytes=vmem),
        name="in_proj",
    )(x2, gain, w_bf16)


def _s5_tables(lam_re, lam_im, log_dt, b_re, b_im, c_re, c_im, d_skip):
    g, p = lam_re.shape
    hg = b_re.shape[-1]
    lc = S5_CHUNK
    lam = lax.complex(lam_re, lam_im)
    dt = jnp.exp(log_dt)[:, None]
    a_bar = jnp.exp(lam * dt)
    b_bar = ((a_bar - 1.0) / lam)[..., None] * lax.complex(b_re, b_im)
    c_mat = lax.complex(c_re, c_im)
    pows = [jnp.ones_like(a_bar)]
    for _ in range(lc):
        pows.append(pows[-1] * a_bar)
    apow = jnp.stack(pows)
    ab = apow[:lc, :, :, None] * b_bar[None]
    kern = (jnp.einsum('ghp,jgpk->jghk', c_re, jnp.real(ab), precision=HIGHEST)
            - jnp.einsum('ghp,jgpk->jghk', c_im, jnp.imag(ab), precision=HIGHEST))
    tau = jnp.arange(lc)[:, None]
    tt = jnp.arange(lc)[None, :]
    lag = tt - tau
    toe = jnp.where((lag >= 0)[:, :, None, None, None], kern[jnp.clip(lag, 0, lc - 1)], 0.0)
    skip = (jnp.eye(lc, dtype=F32)[:, :, None, None, None]
            * (d_skip.reshape(g, hg)[:, :, None] * jnp.eye(hg, dtype=F32)[None])[None, None])
    m = jnp.transpose(toe + skip, (2, 0, 4, 1, 3)).reshape(g, lc * hg, lc * hg)
    bend = apow[lc - 1 - jnp.arange(lc)][:, :, :, None] * b_bar[None]
    bend = jnp.transpose(bend, (1, 0, 3, 2)).reshape(g, lc * hg, p)
    cin = c_mat[None] * apow[1:lc + 1][:, :, None, :]
    cin = jnp.transpose(cin, (1, 3, 0, 2)).reshape(g, p, lc * hg)
    cpows = [apow[lc]]
    for _ in range(S5_ROWS - 1):
        cpows.append(cpows[-1] * apow[lc])
    a_chunk = jnp.stack(cpows, axis=1)

    npair = g // S5_PAIR
    w = lc * hg
    w1 = jnp.zeros((npair, S5_PAIR * w, S5_PAIR * w + 2 * S5_PAIR * p), F32)
    cp = jnp.zeros((npair, 2 * S5_PAIR * p, S5_PAIR * w), F32)
    mp = m.reshape(npair, S5_PAIR, w, w)
    bre = jnp.real(bend).reshape(npair, S5_PAIR, w, p)
    bim = jnp.imag(bend).reshape(npair, S5_PAIR, w, p)
    cre = jnp.real(cin).reshape(npair, S5_PAIR, p, w)
    cim = jnp.imag(cin).reshape(npair, S5_PAIR, p, w)
    e0 = S5_PAIR * w
    for q in range(S5_PAIR):
        w1 = w1.at[:, q * w:(q + 1) * w, q * w:(q + 1) * w].set(mp[:, q])
        w1 = w1.at[:, q * w:(q + 1) * w, e0 + q * p:e0 + (q + 1) * p].set(bre[:, q])
        w1 = w1.at[:, q * w:(q + 1) * w, e0 + S5_PAIR * p + q * p:e0 + S5_PAIR * p + (q + 1) * p].set(bim[:, q])
        cp = cp.at[:, q * p:(q + 1) * p, q * w:(q + 1) * w].set(cre[:, q])
        cp = cp.at[:, S5_PAIR * p + q * p:S5_PAIR * p + (q + 1) * p, q * w:(q + 1) * w].set(-cim[:, q])
    a_chunk = jnp.transpose(a_chunk.reshape(npair, S5_PAIR, S5_ROWS, p), (0, 2, 1, 3))
    a_chunk = a_chunk.reshape(npair, S5_ROWS, S5_PAIR * p)
    a_pair = jnp.stack([jnp.real(a_chunk), jnp.imag(a_chunk)], axis=1)
    return w1.astype(BF16), cp.astype(BF16), a_pair


def _s5_kernel(u_ref, w1_ref, cin_ref, a_ref, z_ref, e_sc, sp_sc, *, n_batch, n_chunks, y_width, lanes):
    u = u_ref[0]
    r = jnp.dot(u, w1_ref[0], preferred_element_type=F32)
    e_sc[...] = r[:, y_width:]
    p_re = a_ref[0, 0]
    p_im = a_ref[0, 1]
    rid = lax.broadcasted_iota(jnp.int32, (S5_ROWS, lanes), 0)

    def shift_rows(v, k, fill):
        return jnp.where(rid >= k, pltpu.roll(v, k, 0), fill)

    def body(i, carry):
        new = []
        for b in range(n_batch):
            c_re, c_im = carry[2 * b], carry[2 * b + 1]
            r0 = pl.multiple_of(b * n_chunks + i * S5_ROWS, S5_ROWS)
            x_re = e_sc[pl.ds(r0, S5_ROWS), 0:lanes]
            x_im = e_sc[pl.ds(r0, S5_ROWS), lanes:2 * lanes]
            k = 1
            while k < S5_ROWS:
                k_re, k_im = p_re[k - 1:k, :], p_im[k - 1:k, :]
                sh_re, sh_im = shift_rows(x_re, k, 0.0), shift_rows(x_im, k, 0.0)
                x_re, x_im = x_re + k_re * sh_re - k_im * sh_im, x_im + k_re * sh_im + k_im * sh_re
                k *= 2
            s_re = x_re + p_re * c_re - p_im * c_im
            s_im = x_im + p_re * c_im + p_im * c_re
            sp_sc[pl.ds(r0, S5_ROWS), 0:lanes] = shift_rows(s_re, 1, c_re)
            sp_sc[pl.ds(r0, S5_ROWS), lanes:2 * lanes] = shift_rows(s_im, 1, c_im)
            new.append(s_re[S5_ROWS - 1:S5_ROWS, :])
            new.append(s_im[S5_ROWS - 1:S5_ROWS, :])
        return tuple(new)

    init = tuple(jnp.zeros((1, lanes), F32) for _ in range(2 * n_batch))
    lax.fori_loop(0, n_chunks // S5_ROWS, body, init, unroll=2)
    y = r[:, :y_width] + jnp.dot(sp_sc[...].astype(BF16), cin_ref[0], preferred_element_type=F32)
    z_ref[0] = jax.nn.gelu(y).astype(BF16)


def _s5_branch(x_b, tables, *, n_batch, seq):
    w1, cp, a_pair = tables
    npair, kw, nw = w1.shape
    lanes = a_pair.shape[-1]
    t, bw = x_b.shape
    lc = S5_CHUNK
    n_chunks = seq // lc
    hg = kw // (S5_PAIR * lc)
    rows = n_batch * n_chunks
    u = x_b.reshape(n_batch, n_chunks, lc, npair, S5_PAIR, hg)
    u = jnp.transpose(u, (3, 0, 1, 4, 2, 5)).reshape(npair, rows, kw)
    z = pl.pallas_call(
        functools.partial(_s5_kernel, n_batch=n_batch, n_chunks=n_chunks, y_width=kw, lanes=lanes),
        out_shape=jax.ShapeDtypeStruct((npair, rows, kw), BF16),
        grid=(npair,),
        in_specs=[
            pl.BlockSpec((1, rows, kw), lambda p: (p, 0, 0)),
            pl.BlockSpec((1, kw, nw), lambda p: (p, 0, 0)),
            pl.BlockSpec((1, 2 * lanes, kw), lambda p: (p, 0, 0)),
            pl.BlockSpec((1, 2, S5_ROWS, lanes), lambda p: (p, 0, 0, 0)),
        ],
        out_specs=pl.BlockSpec((1, rows, kw), lambda p: (p, 0, 0)),
        scratch_shapes=[pltpu.VMEM((rows, 2 * lanes), F32), pltpu.VMEM((rows, 2 * lanes), F32)],
        compiler_params=pltpu.CompilerParams(dimension_semantics=("arbitrary",)),
        name="s5",
    )(u, w1, cp, a_pair)
    z = z.reshape(npair, n_batch, n_chunks, S5_PAIR, lc, hg)
    return jnp.transpose(z, (1, 2, 4, 0, 3, 5)).reshape(t, bw)


def _branches_kernel(u_ref, v_ref, ga_ref, gb_ref, zb_ref, lng_ref, lnb_ref, wsp_ref, bsp_ref,
                     wpa_ref, wga_ref, wgb_ref, o_ref, s_sc, *, n_heads, head_dim):
    j = pl.program_id(1)

    @pl.when(j == 0)
    def _():
        v = v_ref[...].astype(F32)
        mu = jnp.mean(v, axis=-1, keepdims=True)
        vc = v - mu
        var = jnp.mean(vc * vc, axis=-1, keepdims=True)
        vn = (vc * lax.rsqrt(var + NORM_EPS) * lng_ref[...] + lnb_ref[...]).astype(BF16)
        n_c = v.shape[0] // A_CHUNK
        row = lax.broadcasted_iota(jnp.int32, (A_CHUNK, A_CHUNK), 0)
        col = lax.broadcasted_iota(jnp.int32, (A_CHUNK, A_CHUNK), 1)
        causal = row >= col
        for h in range(n_heads):
            hs = slice(h * head_dim, (h + 1) * head_dim)
            w = jnp.where(causal, wsp_ref[h], jnp.zeros((), BF16))
            rhs = jnp.concatenate(
                [vn[c * A_CHUNK:(c + 1) * A_CHUNK, hs] for c in range(n_c)], axis=1)
            mixed = jnp.dot(w, rhs, preferred_element_type=F32)
            bias = bsp_ref[:, hs]
            for c in range(n_c):
                rs = slice(c * A_CHUNK, (c + 1) * A_CHUNK)
                u_blk = u_ref[rs, hs].astype(F32)
                s_sc[rs, hs] = (u_blk * (mixed[:, c * head_dim:(c + 1) * head_dim] + bias)).astype(BF16)

    zb = zb_ref[...]
    br_a = jnp.dot(s_sc[...], wpa_ref[...], preferred_element_type=F32)
    glu_a = jnp.dot(zb, wga_ref[...], preferred_element_type=F32)
    glu_b = jnp.dot(zb, wgb_ref[...], preferred_element_type=F32)
    mix = ga_ref[...].astype(F32) * br_a + gb_ref[...].astype(F32) * (glu_a * jax.nn.sigmoid(glu_b))
    o_ref[...] = mix.astype(BF16)


def _branches(proj, z_b, ln_g, ln_b, w_sp, b_sp_full, wpa, wga, wgb, *, a_width, b_width, d_model,
              tm=512, tn=1024):
    t = proj.shape[0]
    n_heads = w_sp.shape[0]
    head_dim = a_width // n_heads
    assert a_width % tn == 0 or tn % a_width == 0
    ga_off = (2 * a_width + b_width) // tn
    gb_off = (2 * a_width + b_width + d_model) // tn
    assert (2 * a_width + b_width) % tn == 0 and d_model % tn == 0 and t % tm == 0 and tm % A_CHUNK == 0
    vmem = (2 * 2 * tm * a_width * 2 + 2 * 2 * tm * tn * 2 + 2 * tm * b_width * 2
            + 2 * a_width * tn * 2 + 2 * 2 * b_width * tn * 2 + tm * a_width * 2 + 2 * tm * tn * 2
            + 4 * tm * a_width * 4 + 6 * tm * tn * 4)
    return pl.pallas_call(
        functools.partial(_branches_kernel, n_heads=n_heads, head_dim=head_dim),
        out_shape=jax.ShapeDtypeStruct((t, d_model), BF16),
        grid=(t // tm, d_model // tn),
        in_specs=[
            pl.BlockSpec((tm, a_width), lambda i, j: (i, 0)),
            pl.BlockSpec((tm, a_width), lambda i, j: (i, 1)),
            pl.BlockSpec((tm, tn), lambda i, j: (i, ga_off + j)),
            pl.BlockSpec((tm, tn), lambda i, j: (i, gb_off + j)),
            pl.BlockSpec((tm, b_width), lambda i, j: (i, 0)),
            pl.BlockSpec((1, a_width), lambda i, j: (0, 0)),
            pl.BlockSpec((1, a_width), lambda i, j: (0, 0)),
            pl.BlockSpec((n_heads, A_CHUNK, A_CHUNK), lambda i, j: (0, 0, 0)),
            pl.BlockSpec((A_CHUNK, a_width), lambda i, j: (0, 0)),
            pl.BlockSpec((a_width, tn), lambda i, j: (0, j)),
            pl.BlockSpec((b_width, tn), lambda i, j: (0, j)),
            pl.BlockSpec((b_width, tn), lambda i, j: (0, j)),
        ],
        out_specs=pl.BlockSpec((tm, tn), lambda i, j: (i, j)),
        scratch_shapes=[pltpu.VMEM((tm, a_width), BF16)],
        compiler_params=pltpu.CompilerParams(
            dimension_semantics=("arbitrary", "arbitrary"), vmem_limit_bytes=vmem),
        name="branches",
    )(proj, proj, proj, proj, z_b, ln_g, ln_b, w_sp, b_sp_full, wpa, wga, wgb)


def _out_proj_kernel(m_ref, w_ref, x_ref, g_ref, o_ref, acc_sc, *, n_tiles, tn):
    j = pl.program_id(1)
    acc_sc[j] = jnp.dot(m_ref[...], w_ref[...], preferred_element_type=F32)

    @pl.when(j == n_tiles - 1)
    def _():
        ssq = jnp.zeros((acc_sc.shape[1], 1), F32)
        for k in range(n_tiles):
            a = acc_sc[k]
            ssq = ssq + jnp.sum(a * a, axis=-1, keepdims=True)
        inv = lax.rsqrt(ssq / (n_tiles * tn) + NORM_EPS)
        for k in range(n_tiles):
            cs = slice(k * tn, (k + 1) * tn)
            o_ref[:, cs] = x_ref[:, cs] + acc_sc[k] * inv * g_ref[:, cs]


def _out_proj(mix_in, w_bf16, x2, gain, *, tm=256, tn=1024):
    t, d = x2.shape
    n_tiles = d // tn
    vmem = (2 * tm * d * 2 + 2 * d * tn * 2 + 2 * tm * d * 4 + 2 * tm * d * 4 + tm * d * 4 + 4 * tm * tn * 4)
    return pl.pallas_call(
        functools.partial(_out_proj_kernel, n_tiles=n_tiles, tn=tn),
        out_shape=jax.ShapeDtypeStruct((t, d), F32),
        grid=(t // tm, n_tiles),
        in_specs=[
            pl.BlockSpec((tm, d), lambda i, j: (i, 0)),
            pl.BlockSpec((d, tn), lambda i, j: (0, j)),
            pl.BlockSpec((tm, d), lambda i, j: (i, 0)),
            pl.BlockSpec((1, d), lambda i, j: (0, 0)),
        ],
        out_specs=pl.BlockSpec((tm, d), lambda i, j: (i, 0)),
        scratch_shapes=[pltpu.VMEM((n_tiles, tm, tn), F32)],
        compiler_params=pltpu.CompilerParams(
            dimension_semantics=("arbitrary", "arbitrary"), vmem_limit_bytes=vmem),
        name="out_proj",
    )(mix_in, w_bf16, x2, gain)


def _mlp_kernel(x_ref, gpre_ref, wu_ref, wd_ref, gpost_ref, o_ref, h_sc, *, n_f, tn):
    f = pl.program_id(1)

    @pl.when(f == 0)
    def _():
        h_sc[...] = _rms_scale(x_ref[...], gpre_ref[...]).astype(BF16)

    a = jnp.dot(h_sc[...], wu_ref[...], preferred_element_type=F32)
    a = jnp.square(jnp.maximum(a, 0.0)).astype(BF16)
    for k in range(o_ref.shape[1] // tn):
        cs = slice(k * tn, (k + 1) * tn)
        d = jnp.dot(a, wd_ref[:, cs], preferred_element_type=F32)

        @pl.when(f == 0)
        def _():
            o_ref[:, cs] = d

        @pl.when(f > 0)
        def _():
            o_ref[:, cs] += d

    @pl.when(f == n_f - 1)
    def _():
        o_ref[...] = x_ref[...] + _rms_scale(o_ref[...], gpost_ref[...])


def _mlp(x1, g_pre, wu, wd, g_post, *, tm=512, tf=512, tn=1024):
    t, d = x1.shape
    d_ff = wu.shape[1]
    vmem = (tm * d * 4 + 2 * tm * d * 4 + tm * d * 2 + 2 * 2 * d * tf * 2 + 3 * tm * tf * 4 + 2 * tm * tn * 4
            + tm * d * 4)
    return pl.pallas_call(
        functools.partial(_mlp_kernel, n_f=d_ff // tf, tn=tn),
        out_shape=jax.ShapeDtypeStruct((t, d), F32),
        grid=(t // tm, d_ff // tf),
        in_specs=[
            pl.BlockSpec((tm, d), lambda i, f: (i, 0), pipeline_mode=pl.Buffered(1)),
            pl.BlockSpec((1, d), lambda i, f: (0, 0)),
            pl.BlockSpec((d, tf), lambda i, f: (0, f)),
            pl.BlockSpec((tf, d), lambda i, f: (f, 0)),
            pl.BlockSpec((1, d), lambda i, f: (0, 0)),
        ],
        out_specs=pl.BlockSpec((tm, d), lambda i, f: (i, 0)),
        scratch_shapes=[pltpu.VMEM((tm, d), BF16)],
        compiler_params=pltpu.CompilerParams(
            dimension_semantics=("arbitrary", "arbitrary"), vmem_limit_bytes=vmem),
        name="mlp",
    )(x1, g_pre, wu, wd, g_post)


def kernel(x, norm_mix_pre, w_in, v_norm_g, v_norm_b, w_spatial, b_spatial, w_proj_a, lam_re, lam_im,
           log_dt, b_re, b_im, c_re, c_im, d_skip, w_glu_a, w_glu_b, w_out, norm_mix_post, norm_mlp_pre,
           w_ff_up, w_ff_down, norm_mlp_post):
    n_batch, seq, d_model = x.shape
    depth = w_in.shape[0]
    a_width = w_proj_a.shape[1]
    b_width = w_glu_a.shape[1]
    n_heads = w_spatial.shape[1]
    head_dim = a_width // n_heads
    x2 = x.reshape(n_batch * seq, d_model)
    for l in range(depth):
        proj = _in_proj(x2, norm_mix_pre[l][None], w_in[l].astype(BF16),
                        gelu_width=2 * a_width, lin_width=b_width)
        tables = _s5_tables(lam_re[l], lam_im[l], log_dt[l], b_re[l], b_im[l], c_re[l], c_im[l], d_skip[l])
        x_b = proj[:, 2 * a_width:2 * a_width + b_width]
        z_b = _s5_branch(x_b, tables, n_batch=n_batch, seq=seq)
        b_sp_full = jnp.repeat(b_spatial[l].T, head_dim, axis=1)
        mix_in = _branches(proj, z_b, v_norm_g[l][None], v_norm_b[l][None], w_spatial[l].astype(BF16),
                           b_sp_full, w_proj_a[l].astype(BF16), w_glu_a[l].astype(BF16),
                           w_glu_b[l].astype(BF16), a_width=a_width, b_width=b_width, d_model=d_model)
        x1 = _out_proj(mix_in, w_out[l].astype(BF16), x2, norm_mix_post[l][None])
        x2 = _mlp(x1, norm_mlp_pre[l][None], w_ff_up[l].astype(BF16), w_ff_down[l].astype(BF16),
                  norm_mlp_post[l][None])
    return x2.reshape(n_batch, seq, d_model)
```

```python
import functools

import jax
import jax.numpy as jnp
from jax import lax
from jax.experimental import pallas as pl
from jax.experimental.pallas import tpu as pltpu

F32 = jnp.float32
BF16 = jnp.bfloat16
NORM_EPS = 1e-6
HIGHEST = lax.Precision.HIGHEST

LANES = 128
A_CHUNK = 128
S5_CHUNK = 16
S5_PAIR = 2
S5_ROWS = 8


def _rms_scale(x, gain):
    ms = jnp.mean(x * x, axis=-1, keepdims=True)
    return x * lax.rsqrt(ms + NORM_EPS) * gain


def _in_proj_kernel(x_ref, g_ref, w_ref, o_ref, xb_ref, h_sc, *, gelu_tiles):
    j = pl.program_id(1)

    @pl.when(j == 0)
    def _():
        h_sc[...] = _rms_scale(x_ref[...], g_ref[...]).astype(BF16)

    acc = jnp.dot(h_sc[...], w_ref[...], preferred_element_type=F32)

    @pl.when(j < gelu_tiles)
    def _():
        o_ref[...] = jax.nn.gelu(acc).astype(BF16)

    @pl.when(j == gelu_tiles)
    def _():
        o_ref[...] = acc.astype(BF16)
        xb_ref[...] = acc

    @pl.when(j > gelu_tiles)
    def _():
        o_ref[...] = jax.nn.sigmoid(acc).astype(BF16)


def _in_proj(x2, gain, w_bf16, *, gelu_width, lin_width, tm=512, tn=1024):
    t, d = x2.shape
    n = w_bf16.shape[1]
    assert gelu_width % tn == 0 and lin_width == tn and t % tm == 0 and n % tn == 0
    vmem = (2 * tm * d * 4 + tm * d * 2 + 2 * d * tn * 2 + 2 * tm * tn * 2 + 2 * tm * tn * 4
            + 8 * tm * tn * 4)
    return pl.pallas_call(
        functools.partial(_in_proj_kernel, gelu_tiles=gelu_width // tn),
        out_shape=(jax.ShapeDtypeStruct((t, n), BF16), jax.ShapeDtypeStruct((t, lin_width), F32)),
        grid=(t // tm, n // tn),
        in_specs=[
            pl.BlockSpec((tm, d), lambda i, j: (i, 0)),
            pl.BlockSpec((1, d), lambda i, j: (0, 0)),
            pl.BlockSpec((d, tn), lambda i, j: (0, j)),
        ],
        out_specs=(pl.BlockSpec((tm, tn), lambda i, j: (i, j)),
                   pl.BlockSpec((tm, lin_width), lambda i, j: (i, 0))),
        scratch_shapes=[pltpu.VMEM((tm, d), BF16)],
        compiler_params=pltpu.CompilerParams(
            dimension_semantics=("arbitrary", "arbitrary"), vmem_limit_bytes=vmem),
        name="in_proj",
    )(x2, gain, w_bf16)


def _cmul(a, b):
    return a[0] * b[0] - a[1] * b[1], a[0] * b[1] + a[1] * b[0]


def _s5_tables(lam_re, lam_im, log_dt, b_re, b_im, c_re, c_im, d_skip):
    g, p = lam_re.shape
    hg = b_re.shape[-1]
    lc = S5_CHUNK
    npair = g // S5_PAIR
    dt = jnp.exp(log_dt)[:, None]
    mag = jnp.exp(lam_re * dt)
    a_bar = (mag * jnp.cos(lam_im * dt), mag * jnp.sin(lam_im * dt))
    den = lam_re * lam_re + lam_im * lam_im
    gain = (((a_bar[0] - 1.0) * lam_re + a_bar[1] * lam_im) / den,
            (a_bar[1] * lam_re - (a_bar[0] - 1.0) * lam_im) / den)
    b_bar = _cmul((gain[0][..., None], gain[1][..., None]), (b_re, b_im))
    pows = [(jnp.ones_like(mag), jnp.zeros_like(mag))]
    for _ in range(lc):
        pows.append(_cmul(pows[-1], a_bar))
    apow = (jnp.stack([q[0] for q in pows]), jnp.stack([q[1] for q in pows]))
    ab = _cmul((apow[0][:lc, :, :, None], apow[1][:lc, :, :, None]), (b_bar[0][None], b_bar[1][None]))
    kern = (jnp.einsum('ghp,jgpk->jghk', c_re, ab[0], precision=HIGHEST)
            - jnp.einsum('ghp,jgpk->jghk', c_im, ab[1], precision=HIGHEST))
    eye_q = jnp.eye(S5_PAIR, dtype=F32)
    lag = jnp.arange(lc)[None, :] - jnp.arange(lc)[:, None]
    toe = jnp.where((lag >= 0)[:, :, None, None, None], kern[jnp.clip(lag, 0, lc - 1)], 0.0)
    skip = (jnp.eye(lc, dtype=F32)[:, :, None, None, None]
            * (d_skip.reshape(g, hg)[:, :, None] * jnp.eye(hg, dtype=F32)[None])[None, None])
    toe = (toe + skip).reshape(lc, lc, npair, S5_PAIR, hg, hg)
    toe = jnp.transpose(toe, (2, 0, 3, 5, 1, 4))
    w1_y = toe[:, :, :, :, :, None, :] * eye_q[None, None, :, None, None, :, None]
    w1_y = w1_y.reshape(npair, lc * S5_PAIR * hg, lc * S5_PAIR * hg)
    rev = lc - 1 - jnp.arange(lc)
    bend = _cmul((apow[0][rev][:, :, :, None], apow[1][rev][:, :, :, None]), (b_bar[0][None], b_bar[1][None]))
    bend = jnp.stack(bend).reshape(2, lc, npair, S5_PAIR, p, hg)
    bend = jnp.transpose(bend, (2, 1, 3, 5, 0, 4))
    w1_e = bend[:, :, :, :, :, None, :] * eye_q[None, None, :, None, None, :, None]
    w1_e = w1_e.reshape(npair, lc * S5_PAIR * hg, 2 * S5_PAIR * p)
    w1 = jnp.concatenate([w1_y, w1_e], axis=2).astype(BF16)
    cin = _cmul((c_re[None], c_im[None]), (apow[0][1:lc + 1][:, :, None, :], apow[1][1:lc + 1][:, :, None, :]))
    cin = jnp.stack([cin[0], -cin[1]]).reshape(2, lc, npair, S5_PAIR, hg, p)
    cin = jnp.transpose(cin, (2, 0, 3, 5, 1, 4))
    cin = cin[:, :, :, :, :, None, :] * eye_q[None, None, :, None, None, :, None]
    cin = cin.reshape(npair, 2 * S5_PAIR * p, lc * S5_PAIR * hg).astype(BF16)
    a_lc = (apow[0][lc], apow[1][lc])
    cpows = [a_lc]
    for _ in range(S5_ROWS - 1):
        cpows.append(_cmul(cpows[-1], a_lc))
    a_tab = jnp.stack([jnp.stack([q[0] for q in cpows]), jnp.stack([q[1] for q in cpows])])
    a_tab = jnp.transpose(a_tab.reshape(2, S5_ROWS, npair, S5_PAIR * p), (2, 0, 1, 3))
    return w1, cin, a_tab


def _s5_kernel(x_ref, w1_ref, cin_ref, a_ref, z_ref, y_sc, e_sc, sp_sc, *, n_chunks, n_pair, pair_ch):
    lc = S5_CHUNK
    y_width = lc * pair_ch

    for pp in range(n_pair):
        lanes = slice(pp * pair_ch, (pp + 1) * pair_ch)
        u = jnp.concatenate(
            [x_ref[pl.ds(tau, n_chunks, stride=lc), :][:, lanes] for tau in range(lc)], axis=1)
        r = jnp.dot(u.astype(BF16), w1_ref[pp], preferred_element_type=F32)
        y_sc[pp] = r[:, :y_width]
        e_sc[pp] = r[:, y_width:]

    rid = lax.broadcasted_iota(jnp.int32, (S5_ROWS, LANES), 0)

    def shift_rows(v, k, fill):
        return jnp.where(rid >= k, pltpu.roll(v, k, 0), fill)

    def body(i, carry):
        new = []
        r0 = pl.multiple_of(i * S5_ROWS, S5_ROWS)
        for pp in range(n_pair):
            c_re, c_im = carry[2 * pp], carry[2 * pp + 1]
            p_re, p_im = a_ref[pp, 0], a_ref[pp, 1]
            x_re = e_sc[pp, pl.ds(r0, S5_ROWS), 0:LANES]
            x_im = e_sc[pp, pl.ds(r0, S5_ROWS), LANES:2 * LANES]
            k = 1
            while k < S5_ROWS:
                k_re, k_im = p_re[k - 1:k, :], p_im[k - 1:k, :]
                sh_re, sh_im = shift_rows(x_re, k, 0.0), shift_rows(x_im, k, 0.0)
                x_re, x_im = x_re + k_re * sh_re - k_im * sh_im, x_im + k_re * sh_im + k_im * sh_re
                k *= 2
            s_re = x_re + p_re * c_re - p_im * c_im
            s_im = x_im + p_re * c_im + p_im * c_re
            sp_sc[pp, pl.ds(r0, S5_ROWS), 0:LANES] = shift_rows(s_re, 1, c_re)
            sp_sc[pp, pl.ds(r0, S5_ROWS), LANES:2 * LANES] = shift_rows(s_im, 1, c_im)
            new.append(s_re[S5_ROWS - 1:S5_ROWS, :])
            new.append(s_im[S5_ROWS - 1:S5_ROWS, :])
        return tuple(new)

    init = tuple(jnp.zeros((1, LANES), F32) for _ in range(2 * n_pair))
    lax.fori_loop(0, n_chunks // S5_ROWS, body, init)

    ys = [jax.nn.gelu(y_sc[pp] + jnp.dot(sp_sc[pp].astype(BF16), cin_ref[pp], preferred_element_type=F32))
          for pp in range(n_pair)]
    for t in range(lc):
        z_t = jnp.concatenate([y[:, t * pair_ch:(t + 1) * pair_ch] for y in ys], axis=1)
        z_ref[pl.ds(t, n_chunks, stride=lc), :] = z_t


def _s5_branch(x_b, tables, *, n_batch, seq):
    w1, cin, a_tab = tables
    npair, kw, nw = w1.shape
    t, bw = x_b.shape
    lc = S5_CHUNK
    n_chunks = seq // lc
    pair_ch = kw // lc
    n_pair = LANES // pair_ch
    assert n_chunks % S5_ROWS == 0 and bw % LANES == 0
    return pl.pallas_call(
        functools.partial(_s5_kernel, n_chunks=n_chunks, n_pair=n_pair, pair_ch=pair_ch),
        out_shape=jax.ShapeDtypeStruct((t, bw), F32),
        grid=(bw // LANES, n_batch),
        in_specs=[
            pl.BlockSpec((seq, LANES), lambda c, b: (b, c)),
            pl.BlockSpec((n_pair, kw, nw), lambda c, b: (c, 0, 0)),
            pl.BlockSpec((n_pair, 2 * LANES, kw), lambda c, b: (c, 0, 0)),
            pl.BlockSpec((n_pair, 2, S5_ROWS, LANES), lambda c, b: (c, 0, 0, 0)),
        ],
        out_specs=pl.BlockSpec((seq, LANES), lambda c, b: (b, c)),
        scratch_shapes=[pltpu.VMEM((n_pair, n_chunks, kw), F32),
                        pltpu.VMEM((n_pair, n_chunks, 2 * LANES), F32),
                        pltpu.VMEM((n_pair, n_chunks, 2 * LANES), F32)],
        compiler_params=pltpu.CompilerParams(dimension_semantics=("arbitrary", "arbitrary")),
        name="s5",
    )(x_b, w1, cin, a_tab)


def _branches_kernel(u_ref, v_ref, ga_ref, gb_ref, zb_ref, lng_ref, lnb_ref, wsp_ref, bsp_ref,
                     wpa_ref, wga_ref, wgb_ref, o_ref, s_sc, *, n_heads, head_dim):
    j = pl.program_id(1)

    @pl.when(j == 0)
    def _():
        v = v_ref[...].astype(F32)
        mu = jnp.mean(v, axis=-1, keepdims=True)
        vc = v - mu
        var = jnp.mean(vc * vc, axis=-1, keepdims=True)
        vn = (vc * lax.rsqrt(var + NORM_EPS) * lng_ref[...] + lnb_ref[...]).astype(BF16)
        n_c = v.shape[0] // A_CHUNK
        row = lax.broadcasted_iota(jnp.int32, (A_CHUNK, A_CHUNK), 0)
        col = lax.broadcasted_iota(jnp.int32, (A_CHUNK, A_CHUNK), 1)
        causal = row >= col
        for h in range(n_heads):
            hs = slice(h * head_dim, (h + 1) * head_dim)
            w = jnp.where(causal, wsp_ref[h], jnp.zeros((), BF16))
            rhs = jnp.concatenate(
                [vn[c * A_CHUNK:(c + 1) * A_CHUNK, hs] for c in range(n_c)], axis=1)
            mixed = jnp.dot(w, rhs, preferred_element_type=F32)
            bias = bsp_ref[:, hs]
            for c in range(n_c):
                rs = slice(c * A_CHUNK, (c + 1) * A_CHUNK)
                u_blk = u_ref[rs, hs].astype(F32)
                s_sc[rs, hs] = (u_blk * (mixed[:, c * head_dim:(c + 1) * head_dim] + bias)).astype(BF16)

    zb = zb_ref[...].astype(BF16)
    br_a = jnp.dot(s_sc[...], wpa_ref[...], preferred_element_type=F32)
    glu_a = jnp.dot(zb, wga_ref[...], preferred_element_type=F32)
    glu_b = jnp.dot(zb, wgb_ref[...], preferred_element_type=F32)
    mix = ga_ref[...].astype(F32) * br_a + gb_ref[...].astype(F32) * (glu_a * jax.nn.sigmoid(glu_b))
    o_ref[...] = mix.astype(BF16)


def _branches(proj, z_b, ln_g, ln_b, w_sp, b_sp_full, wpa, wga, wgb, *, a_width, b_width, d_model,
              tm=512, tn=1024):
    t = proj.shape[0]
    n_heads = w_sp.shape[0]
    head_dim = a_width // n_heads
    ga_off = (2 * a_width + b_width) // tn
    gb_off = (2 * a_width + b_width + d_model) // tn
    assert (2 * a_width + b_width) % tn == 0 and d_model % tn == 0 and t % tm == 0 and tm % A_CHUNK == 0
    vmem = (2 * 2 * tm * a_width * 2 + 2 * 2 * tm * tn * 2 + 2 * tm * b_width * 4
            + 2 * a_width * tn * 2 + 2 * 2 * b_width * tn * 2 + tm * a_width * 2 + 2 * tm * tn * 2
            + 4 * tm * a_width * 4 + 6 * tm * tn * 4)
    return pl.pallas_call(
        functools.partial(_branches_kernel, n_heads=n_heads, head_dim=head_dim),
        out_shape=jax.ShapeDtypeStruct((t, d_model), BF16),
        grid=(t // tm, d_model // tn),
        in_specs=[
            pl.BlockSpec((tm, a_width), lambda i, j: (i, 0)),
            pl.BlockSpec((tm, a_width), lambda i, j: (i, 1)),
            pl.BlockSpec((tm, tn), lambda i, j: (i, ga_off + j)),
            pl.BlockSpec((tm, tn), lambda i, j: (i, gb_off + j)),
            pl.BlockSpec((tm, b_width), lambda i, j: (i, 0)),
            pl.BlockSpec((1, a_width), lambda i, j: (0, 0)),
            pl.BlockSpec((1, a_width), lambda i, j: (0, 0)),
            pl.BlockSpec((n_heads, A_CHUNK, A_CHUNK), lambda i, j: (0, 0, 0)),
            pl.BlockSpec((A_CHUNK, a_width), lambda i, j: (0, 0)),
            pl.BlockSpec((a_width, tn), lambda i, j: (0, j)),
            pl.BlockSpec((b_width, tn), lambda i, j: (0, j)),
            pl.BlockSpec((b_width, tn), lambda i, j: (0, j)),
        ],
        out_specs=pl.BlockSpec((tm, tn), lambda i, j: (i, j)),
        scratch_shapes=[pltpu.VMEM((tm, a_width), BF16)],
        compiler_params=pltpu.CompilerParams(
            dimension_semantics=("arbitrary", "arbitrary"), vmem_limit_bytes=vmem),
        name="branches",
    )(proj, proj, proj, proj, z_b, ln_g, ln_b, w_sp, b_sp_full, wpa, wga, wgb)


def _out_proj_kernel(m_ref, w_ref, x_ref, g_ref, o_ref, acc_sc, *, n_tiles, tn):
    j = pl.program_id(1)
    acc_sc[j] = jnp.dot(m_ref[...], w_ref[...], preferred_element_type=F32)

    @pl.when(j == n_tiles - 1)
    def _():
        ssq = jnp.zeros((acc_sc.shape[1], 1), F32)
        for k in range(n_tiles):
            a = acc_sc[k]
            ssq = ssq + jnp.sum(a * a, axis=-1, keepdims=True)
        inv = lax.rsqrt(ssq / (n_tiles * tn) + NORM_EPS)
        for k in range(n_tiles):
            cs = slice(k * tn, (k + 1) * tn)
            o_ref[:, cs] = x_ref[:, cs] + acc_sc[k] * inv * g_ref[:, cs]


def _out_proj(mix_in, w_bf16, x2, gain, *, tm=256, tn=1024):
    t, d = x2.shape
    n_tiles = d // tn
    vmem = (2 * tm * d * 2 + 2 * d * tn * 2 + 2 * tm * d * 4 + 2 * tm * d * 4 + tm * d * 4 + 4 * tm * tn * 4)
    return pl.pallas_call(
        functools.partial(_out_proj_kernel, n_tiles=n_tiles, tn=tn),
        out_shape=jax.ShapeDtypeStruct((t, d), F32),
        grid=(t // tm, n_tiles),
        in_specs=[
            pl.BlockSpec((tm, d), lambda i, j: (i, 0)),
            pl.BlockSpec((d, tn), lambda i, j: (0, j)),
            pl.BlockSpec((tm, d), lambda i, j: (i, 0)),
            pl.BlockSpec((1, d), lambda i, j: (0, 0)),
        ],
        out_specs=pl.BlockSpec((tm, d), lambda i, j: (i, 0)),
        scratch_shapes=[pltpu.VMEM((n_tiles, tm, tn), F32)],
        compiler_params=pltpu.CompilerParams(
            dimension_semantics=("arbitrary", "arbitrary"), vmem_limit_bytes=vmem),
        name="out_proj",
    )(mix_in, w_bf16, x2, gain)


def _mlp_kernel(x_ref, gpre_ref, wu_ref, wd_ref, gpost_ref, o_ref, h_sc, a_sc, *, n_f, tn):
    s = pl.program_id(1)
    cur = s % 2

    def up(slot):
        a = jnp.dot(h_sc[...], wu_ref[...], preferred_element_type=F32)
        a_sc[slot] = jnp.square(jnp.maximum(a, 0.0)).astype(BF16)

    def down(slot):
        a = a_sc[slot]
        for k in range(o_ref.shape[1] // tn):
            cs = slice(k * tn, (k + 1) * tn)
            o_ref[:, cs] += jnp.dot(a, wd_ref[:, cs], preferred_element_type=F32)

    @pl.when(s == 0)
    def _():
        h_sc[...] = _rms_scale(x_ref[...], gpre_ref[...]).astype(BF16)
        o_ref[...] = jnp.zeros(o_ref.shape, F32)
        up(cur)

    @pl.when(jnp.logical_and(s > 0, s < n_f))
    def _():
        down(1 - cur)
        up(cur)

    @pl.when(s == n_f)
    def _():
        down(1 - cur)
        o_ref[...] = x_ref[...] + _rms_scale(o_ref[...], gpost_ref[...])


def _mlp(x1, g_pre, wu, wd, g_post, *, tm=512, tf=512, tn=1024):
    t, d = x1.shape
    d_ff = wu.shape[1]
    n_f = d_ff // tf
    vmem = (tm * d * 4 + 2 * tm * d * 4 + tm * d * 2 + 2 * 2 * d * tf * 2 + 2 * tm * tf * 2
            + 3 * tm * tf * 4 + 2 * tm * tn * 4 + tm * d * 4)
    return pl.pallas_call(
        functools.partial(_mlp_kernel, n_f=n_f, tn=tn),
        out_shape=jax.ShapeDtypeStruct((t, d), F32),
        grid=(t // tm, n_f + 1),
        in_specs=[
            pl.BlockSpec((tm, d), lambda i, s: (i, 0), pipeline_mode=pl.Buffered(1)),
            pl.BlockSpec((1, d), lambda i, s: (0, 0)),
            pl.BlockSpec((d, tf), lambda i, s: (0, jnp.minimum(s, n_f - 1))),
            pl.BlockSpec((tf, d), lambda i, s: (jnp.maximum(s - 1, 0), 0)),
            pl.BlockSpec((1, d), lambda i, s: (0, 0)),
        ],
        out_specs=pl.BlockSpec((tm, d), lambda i, s: (i, 0)),
        scratch_shapes=[pltpu.VMEM((tm, d), BF16), pltpu.VMEM((2, tm, tf), BF16)],
        compiler_params=pltpu.CompilerParams(
            dimension_semantics=("arbitrary", "arbitrary"), vmem_limit_bytes=vmem),
        name="mlp",
    )(x1, g_pre, wu, wd, g_post)


def kernel(x, norm_mix_pre, w_in, v_norm_g, v_norm_b, w_spatial, b_spatial, w_proj_a, lam_re, lam_im,
           log_dt, b_re, b_im, c_re, c_im, d_skip, w_glu_a, w_glu_b, w_out, norm_mix_post, norm_mlp_pre,
           w_ff_up, w_ff_down, norm_mlp_post):
    n_batch, seq, d_model = x.shape
    depth = w_in.shape[0]
    a_width = w_proj_a.shape[1]
    b_width = w_glu_a.shape[1]
    n_heads = w_spatial.shape[1]
    head_dim = a_width // n_heads
    x2 = x.reshape(n_batch * seq, d_model)
    for l in range(depth):
        proj, x_b = _in_proj(x2, norm_mix_pre[l][None], w_in[l].astype(BF16),
                             gelu_width=2 * a_width, lin_width=b_width)
        tables = _s5_tables(lam_re[l], lam_im[l], log_dt[l], b_re[l], b_im[l], c_re[l], c_im[l], d_skip[l])
        z_b = _s5_branch(x_b, tables, n_batch=n_batch, seq=seq)
        b_sp_full = jnp.repeat(b_spatial[l].T, head_dim, axis=1)
        mix_in = _branches(proj, z_b, v_norm_g[l][None], v_norm_b[l][None], w_spatial[l].astype(BF16),
                           b_sp_full, w_proj_a[l].astype(BF16), w_glu_a[l].astype(BF16),
                           w_glu_b[l].astype(BF16), a_width=a_width, b_width=b_width, d_model=d_model)
        x1 = _out_proj(mix_in, w_out[l].astype(BF16), x2, norm_mix_post[l][None])
        x2 = _mlp(x1, norm_mlp_pre[l][None], w_ff_up[l].astype(BF16), w_ff_down[l].astype(BF16),
                  norm_mlp_post[l][None])
    return x2.reshape(n_batch, seq, d_model)
```

```python
import functools

import jax
import jax.numpy as jnp
from jax import lax
from jax.experimental import pallas as pl
from jax.experimental.pallas import tpu as pltpu

F32 = jnp.float32
BF16 = jnp.bfloat16
NORM_EPS = 1e-6
HIGHEST = lax.Precision.HIGHEST

LANES = 128
A_CHUNK = 128
S5_CHUNK = 16
S5_PAIR = 2
S5_ROWS = 8


def _rms_scale(x, gain):
    ms = jnp.mean(x * x, axis=-1, keepdims=True)
    return x * lax.rsqrt(ms + NORM_EPS) * gain


def _cast_rider(w, n_i, n_j):
    r, c = w.shape
    blk = (r // n_i, c // n_j)
    assert r % n_i == 0 and c % n_j == 0 and blk[0] % 16 == 0 and blk[1] % LANES == 0

    def index(i, j):
        return i, jnp.minimum(j, n_j - 1)

    return pl.BlockSpec(blk, index), pl.BlockSpec(blk, index), jax.ShapeDtypeStruct(w.shape, BF16)


def _rider_bytes(w, n_i, n_j):
    return 2 * (w.size // (n_i * n_j)) * (4 + 2)


def _in_proj_kernel(*refs, gelu_tiles, n_riders):
    x_ref, g_ref, w_ref = refs[:3]
    cast_in = refs[3:3 + n_riders]
    o_ref, xb_ref = refs[3 + n_riders:5 + n_riders]
    cast_out = refs[5 + n_riders:5 + 2 * n_riders]
    h_sc = refs[5 + 2 * n_riders]
    j = pl.program_id(1)

    @pl.when(j == 0)
    def _():
        h_sc[...] = _rms_scale(x_ref[...], g_ref[...]).astype(BF16)

    for src, dst in zip(cast_in, cast_out):
        dst[...] = src[...].astype(BF16)

    acc = jnp.dot(h_sc[...], w_ref[...], preferred_element_type=F32)

    @pl.when(j < gelu_tiles)
    def _():
        o_ref[...] = jax.nn.gelu(acc).astype(BF16)

    @pl.when(j == gelu_tiles)
    def _():
        o_ref[...] = acc.astype(BF16)
        xb_ref[...] = acc

    @pl.when(j > gelu_tiles)
    def _():
        o_ref[...] = jax.nn.sigmoid(acc).astype(BF16)


def _in_proj(x2, gain, w_bf16, riders, *, gelu_width, lin_width, tm=512, tn=1024, rider_cols=8):
    t, d = x2.shape
    n = w_bf16.shape[1]
    n_tiles = n // tn
    assert gelu_width % tn == 0 and lin_width == tn and t % tm == 0 and n % tn == 0 and n_tiles >= rider_cols
    n_i = t // tm
    rider_specs = [_cast_rider(w, n_i, rider_cols) for w in riders]
    vmem = (2 * tm * d * 4 + tm * d * 2 + 2 * d * tn * 2 + 2 * tm * tn * 2 + 2 * tm * tn * 4
            + 8 * tm * tn * 4 + sum(_rider_bytes(w, n_i, rider_cols) for w in riders))
    outs = pl.pallas_call(
        functools.partial(_in_proj_kernel, gelu_tiles=gelu_width // tn, n_riders=len(riders)),
        out_shape=(jax.ShapeDtypeStruct((t, n), BF16), jax.ShapeDtypeStruct((t, lin_width), F32),
                   *[s[2] for s in rider_specs]),
        grid=(n_i, n_tiles),
        in_specs=[
            pl.BlockSpec((tm, d), lambda i, j: (i, 0)),
            pl.BlockSpec((1, d), lambda i, j: (0, 0)),
            pl.BlockSpec((d, tn), lambda i, j: (0, j)),
            *[s[0] for s in rider_specs],
        ],
        out_specs=(pl.BlockSpec((tm, tn), lambda i, j: (i, j)),
                   pl.BlockSpec((tm, lin_width), lambda i, j: (i, 0)),
                   *[s[1] for s in rider_specs]),
        scratch_shapes=[pltpu.VMEM((tm, d), BF16)],
        compiler_params=pltpu.CompilerParams(
            dimension_semantics=("arbitrary", "arbitrary"), vmem_limit_bytes=vmem),
        name="in_proj",
    )(x2, gain, w_bf16, *riders)
    return outs[0], outs[1], outs[2:]


def _cmul(a, b):
    return a[0] * b[0] - a[1] * b[1], a[0] * b[1] + a[1] * b[0]


def _s5_tables(lam_re, lam_im, log_dt, b_re, b_im, c_re, c_im, d_skip):
    g, p = lam_re.shape
    hg = b_re.shape[-1]
    lc = S5_CHUNK
    npair = g // S5_PAIR
    dt = jnp.exp(log_dt)[:, None]
    mag = jnp.exp(lam_re * dt)
    a_bar = (mag * jnp.cos(lam_im * dt), mag * jnp.sin(lam_im * dt))
    den = lam_re * lam_re + lam_im * lam_im
    gain = (((a_bar[0] - 1.0) * lam_re + a_bar[1] * lam_im) / den,
            (a_bar[1] * lam_re - (a_bar[0] - 1.0) * lam_im) / den)
    b_bar = _cmul((gain[0][..., None], gain[1][..., None]), (b_re, b_im))
    pows = [(jnp.ones_like(mag), jnp.zeros_like(mag))]
    for _ in range(lc):
        pows.append(_cmul(pows[-1], a_bar))
    apow = (jnp.stack([q[0] for q in pows]), jnp.stack([q[1] for q in pows]))
    lanes = S5_PAIR * p
    own = (jnp.arange(S5_PAIR)[:, None] == (jnp.arange(lanes) // p)[None, :]).astype(F32)

    def pair_lanes(v):
        return jnp.transpose(v.reshape(v.shape[0], npair, lanes), (1, 0, 2))

    ap = (pair_lanes(apow[0]), pair_lanes(apow[1]))
    bbt = [jnp.transpose(v.reshape(npair, S5_PAIR, p, hg), (0, 3, 1, 2)).reshape(npair, 1, hg, lanes)
           * own[None, :, None, :] for v in b_bar]
    e = _cmul((ap[0][:, lc - 1::-1][:, :, None, None, :], ap[1][:, lc - 1::-1][:, :, None, None, :]),
              (bbt[0][:, None], bbt[1][:, None]))
    w1_e = jnp.concatenate(e, axis=-1).reshape(npair, lc * S5_PAIR * hg, 2 * lanes)
    cm = [jnp.transpose(v.reshape(npair, S5_PAIR, hg, p), (0, 1, 3, 2))[:, :, :, None, :]
          * jnp.eye(S5_PAIR, dtype=F32)[None, :, None, :, None] for v in (c_re, -c_im)]
    cm = jnp.concatenate([v.reshape(npair, lanes, S5_PAIR * hg) for v in cm], axis=1)
    kr = jnp.einsum('pxl,plo->pxo', w1_e, cm, precision=HIGHEST)
    n_blk = S5_PAIR * hg
    skip = d_skip.reshape(npair, n_blk)[:, :, None] * jnp.eye(n_blk, dtype=F32)[None]
    kr = jnp.concatenate([kr[:, :(lc - 1) * n_blk], kr[:, (lc - 1) * n_blk:] + skip], axis=1)
    kr = jnp.pad(kr, ((0, 0), (0, (lc - 1) * n_blk), (0, 0)))
    w1_y = jnp.concatenate(
        [kr[:, (lc - 1 - t) * n_blk:(2 * lc - 1 - t) * n_blk] for t in range(lc)], axis=-1)
    w1 = jnp.concatenate([w1_y, w1_e], axis=-1).astype(BF16)
    ct = [jnp.tile(v.reshape(npair, S5_PAIR, hg, p), (1, 1, 1, S5_PAIR)) * own[None, :, None, :]
          for v in (c_re, c_im)]
    cin = _cmul((ct[0][:, None], ct[1][:, None]),
                (ap[0][:, 1:lc + 1][:, :, None, None, :], ap[1][:, 1:lc + 1][:, :, None, None, :]))
    cin = jnp.concatenate([cin[0], -cin[1]], axis=-1).reshape(npair, lc * n_blk, 2 * lanes)
    cin = jnp.swapaxes(cin.astype(BF16), 1, 2)
    a_lc = (apow[0][lc], apow[1][lc])
    cpows = [a_lc]
    for _ in range(S5_ROWS - 1):
        cpows.append(_cmul(cpows[-1], a_lc))
    a_tab = jnp.stack([jnp.stack([q[0] for q in cpows]), jnp.stack([q[1] for q in cpows])])
    a_tab = jnp.transpose(a_tab.reshape(2, S5_ROWS, npair, S5_PAIR * p), (2, 0, 1, 3))
    return w1, cin, a_tab


def _s5_kernel(x_ref, w1_ref, cin_ref, a_ref, z_ref, y_sc, e_sc, sp_sc, *, n_chunks, n_pair, pair_ch):
    lc = S5_CHUNK
    y_width = lc * pair_ch

    for pp in range(n_pair):
        lanes = slice(pp * pair_ch, (pp + 1) * pair_ch)
        u = jnp.concatenate(
            [x_ref[pl.ds(tau, n_chunks, stride=lc), :][:, lanes] for tau in range(lc)], axis=1)
        r = jnp.dot(u.astype(BF16), w1_ref[pp], preferred_element_type=F32)
        y_sc[pp] = r[:, :y_width]
        e_sc[pp] = r[:, y_width:]

    rid = lax.broadcasted_iota(jnp.int32, (S5_ROWS, LANES), 0)

    def shift_rows(v, k, fill):
        return jnp.where(rid >= k, pltpu.roll(v, k, 0), fill)

    def body(i, carry):
        new = []
        r0 = pl.multiple_of(i * S5_ROWS, S5_ROWS)
        for pp in range(n_pair):
            c_re, c_im = carry[2 * pp], carry[2 * pp + 1]
            p_re, p_im = a_ref[pp, 0], a_ref[pp, 1]
            x_re = e_sc[pp, pl.ds(r0, S5_ROWS), 0:LANES]
            x_im = e_sc[pp, pl.ds(r0, S5_ROWS), LANES:2 * LANES]
            k = 1
            while k < S5_ROWS:
                k_re, k_im = p_re[k - 1:k, :], p_im[k - 1:k, :]
                sh_re, sh_im = shift_rows(x_re, k, 0.0), shift_rows(x_im, k, 0.0)
                x_re, x_im = x_re + k_re * sh_re - k_im * sh_im, x_im + k_re * sh_im + k_im * sh_re
                k *= 2
            s_re = x_re + p_re * c_re - p_im * c_im
            s_im = x_im + p_re * c_im + p_im * c_re
            sp_sc[pp, pl.ds(r0, S5_ROWS), 0:LANES] = shift_rows(s_re, 1, c_re)
            sp_sc[pp, pl.ds(r0, S5_ROWS), LANES:2 * LANES] = shift_rows(s_im, 1, c_im)
            new.append(s_re[S5_ROWS - 1:S5_ROWS, :])
            new.append(s_im[S5_ROWS - 1:S5_ROWS, :])
        return tuple(new)

    init = tuple(jnp.zeros((1, LANES), F32) for _ in range(2 * n_pair))
    lax.fori_loop(0, n_chunks // S5_ROWS, body, init)

    ys = [jax.nn.gelu(y_sc[pp] + jnp.dot(sp_sc[pp].astype(BF16), cin_ref[pp], preferred_element_type=F32))
          for pp in range(n_pair)]
    for t in range(lc):
        z_t = jnp.concatenate([y[:, t * pair_ch:(t + 1) * pair_ch] for y in ys], axis=1)
        z_ref[pl.ds(t, n_chunks, stride=lc), :] = z_t


def _s5_branch(x_b, tables, *, n_batch, seq):
    w1, cin, a_tab = tables
    npair, kw, nw = w1.shape
    t, bw = x_b.shape
    lc = S5_CHUNK
    n_chunks = seq // lc
    pair_ch = kw // lc
    n_pair = LANES // pair_ch
    assert n_chunks % S5_ROWS == 0 and bw % LANES == 0
    return pl.pallas_call(
        functools.partial(_s5_kernel, n_chunks=n_chunks, n_pair=n_pair, pair_ch=pair_ch),
        out_shape=jax.ShapeDtypeStruct((t, bw), F32),
        grid=(bw // LANES, n_batch),
        in_specs=[
            pl.BlockSpec((seq, LANES), lambda c, b: (b, c)),
            pl.BlockSpec((n_pair, kw, nw), lambda c, b: (c, 0, 0)),
            pl.BlockSpec((n_pair, 2 * LANES, kw), lambda c, b: (c, 0, 0)),
            pl.BlockSpec((n_pair, 2, S5_ROWS, LANES), lambda c, b: (c, 0, 0, 0)),
        ],
        out_specs=pl.BlockSpec((seq, LANES), lambda c, b: (b, c)),
        scratch_shapes=[pltpu.VMEM((n_pair, n_chunks, kw), F32),
                        pltpu.VMEM((n_pair, n_chunks, 2 * LANES), F32),
                        pltpu.VMEM((n_pair, n_chunks, 2 * LANES), F32)],
        compiler_params=pltpu.CompilerParams(dimension_semantics=("arbitrary", "arbitrary")),
        name="s5",
    )(x_b, w1, cin, a_tab)


def _branches_kernel(u_ref, v_ref, ga_ref, gb_ref, zb_ref, lng_ref, lnb_ref, wsp_ref, bsp_ref,
                     wpa_ref, wga_ref, wgb_ref, cast_in_ref, o_ref, cast_out_ref, s_sc, *, n_heads, head_dim):
    j = pl.program_id(1)
    cast_out_ref[...] = cast_in_ref[...].astype(BF16)

    @pl.when(j == 0)
    def _():
        v = v_ref[...].astype(F32)
        mu = jnp.mean(v, axis=-1, keepdims=True)
        vc = v - mu
        var = jnp.mean(vc * vc, axis=-1, keepdims=True)
        vn = (vc * lax.rsqrt(var + NORM_EPS) * lng_ref[...] + lnb_ref[...]).astype(BF16)
        n_c = v.shape[0] // A_CHUNK
        row = lax.broadcasted_iota(jnp.int32, (A_CHUNK, A_CHUNK), 0)
        col = lax.broadcasted_iota(jnp.int32, (A_CHUNK, A_CHUNK), 1)
        causal = row >= col
        for h in range(n_heads):
            hs = slice(h * head_dim, (h + 1) * head_dim)
            w = jnp.where(causal, wsp_ref[h], jnp.zeros((), BF16))
            rhs = jnp.concatenate(
                [vn[c * A_CHUNK:(c + 1) * A_CHUNK, hs] for c in range(n_c)], axis=1)
            mixed = jnp.dot(w, rhs, preferred_element_type=F32)
            bias = bsp_ref[:, hs]
            for c in range(n_c):
                rs = slice(c * A_CHUNK, (c + 1) * A_CHUNK)
                u_blk = u_ref[rs, hs].astype(F32)
                s_sc[rs, hs] = (u_blk * (mixed[:, c * head_dim:(c + 1) * head_dim] + bias)).astype(BF16)

    zb = zb_ref[...].astype(BF16)
    br_a = jnp.dot(s_sc[...], wpa_ref[...], preferred_element_type=F32)
    glu_a = jnp.dot(zb, wga_ref[...], preferred_element_type=F32)
    glu_b = jnp.dot(zb, wgb_ref[...], preferred_element_type=F32)
    mix = ga_ref[...].astype(F32) * br_a + gb_ref[...].astype(F32) * (glu_a * jax.nn.sigmoid(glu_b))
    o_ref[...] = mix.astype(BF16)


def _branches(proj, z_b, ln_g, ln_b, w_sp, b_sp_full, wpa, wga, wgb, rider, *, a_width, b_width, d_model,
              tm=512, tn=1024):
    t = proj.shape[0]
    n_heads = w_sp.shape[0]
    head_dim = a_width // n_heads
    ga_off = (2 * a_width + b_width) // tn
    gb_off = (2 * a_width + b_width + d_model) // tn
    assert (2 * a_width + b_width) % tn == 0 and d_model % tn == 0 and t % tm == 0 and tm % A_CHUNK == 0
    n_i, n_j = t // tm, d_model // tn
    rider_in, rider_out, rider_shape = _cast_rider(rider, n_i, n_j)
    vmem = (2 * 2 * tm * a_width * 2 + 2 * 2 * tm * tn * 2 + 2 * tm * b_width * 4
            + 2 * a_width * tn * 2 + 2 * 2 * b_width * tn * 2 + tm * a_width * 2 + 2 * tm * tn * 2
            + 4 * tm * a_width * 4 + 6 * tm * tn * 4 + _rider_bytes(rider, n_i, n_j))
    return pl.pallas_call(
        functools.partial(_branches_kernel, n_heads=n_heads, head_dim=head_dim),
        out_shape=(jax.ShapeDtypeStruct((t, d_model), BF16), rider_shape),
        grid=(n_i, n_j),
        in_specs=[
            pl.BlockSpec((tm, a_width), lambda i, j: (i, 0)),
            pl.BlockSpec((tm, a_width), lambda i, j: (i, 1)),
            pl.BlockSpec((tm, tn), lambda i, j: (i, ga_off + j)),
            pl.BlockSpec((tm, tn), lambda i, j: (i, gb_off + j)),
            pl.BlockSpec((tm, b_width), lambda i, j: (i, 0)),
            pl.BlockSpec((1, a_width), lambda i, j: (0, 0)),
            pl.BlockSpec((1, a_width), lambda i, j: (0, 0)),
            pl.BlockSpec((n_heads, A_CHUNK, A_CHUNK), lambda i, j: (0, 0, 0)),
            pl.BlockSpec((A_CHUNK, a_width), lambda i, j: (0, 0)),
            pl.BlockSpec((a_width, tn), lambda i, j: (0, j)),
            pl.BlockSpec((b_width, tn), lambda i, j: (0, j)),
            pl.BlockSpec((b_width, tn), lambda i, j: (0, j)),
            rider_in,
        ],
        out_specs=(pl.BlockSpec((tm, tn), lambda i, j: (i, j)), rider_out),
        scratch_shapes=[pltpu.VMEM((tm, a_width), BF16)],
        compiler_params=pltpu.CompilerParams(
            dimension_semantics=("arbitrary", "arbitrary"), vmem_limit_bytes=vmem),
        name="branches",
    )(proj, proj, proj, proj, z_b, ln_g, ln_b, w_sp, b_sp_full, wpa, wga, wgb, rider)


def _out_proj_kernel(m_ref, w_ref, x_ref, g_ref, o_ref, x_sc, *, n_tiles, tn):
    j = pl.program_id(1)
    for k in range(n_tiles):
        @pl.when(j == k)
        def _(k=k):
            cs = slice(k * tn, (k + 1) * tn)
            o_ref[:, cs] = jnp.dot(m_ref[...], w_ref[...], preferred_element_type=F32)
            x_sc[:, cs] = x_ref[...]

    @pl.when(j == n_tiles - 1)
    def _():
        ssq = jnp.zeros((o_ref.shape[0], 1), F32)
        for k in range(n_tiles):
            a = o_ref[:, k * tn:(k + 1) * tn]
            ssq = ssq + jnp.sum(a * a, axis=-1, keepdims=True)
        inv = lax.rsqrt(ssq / (n_tiles * tn) + NORM_EPS)
        for k in range(n_tiles):
            cs = slice(k * tn, (k + 1) * tn)
            o_ref[:, cs] = x_sc[:, cs] + o_ref[:, cs] * inv * g_ref[:, cs]


def _out_proj(mix_in, w_bf16, x2, gain, *, tm=512, tn=1024):
    t, d = x2.shape
    n_tiles = d // tn
    vmem = (2 * tm * d * 2 + 2 * d * tn * 2 + 2 * tm * tn * 4 + 2 * tm * d * 4 + tm * d * 4 + 4 * tm * tn * 4)
    return pl.pallas_call(
        functools.partial(_out_proj_kernel, n_tiles=n_tiles, tn=tn),
        out_shape=jax.ShapeDtypeStruct((t, d), F32),
        grid=(t // tm, n_tiles),
        in_specs=[
            pl.BlockSpec((tm, d), lambda i, j: (i, 0)),
            pl.BlockSpec((d, tn), lambda i, j: (0, j)),
            pl.BlockSpec((tm, tn), lambda i, j: (i, j)),
            pl.BlockSpec((1, d), lambda i, j: (0, 0)),
        ],
        out_specs=pl.BlockSpec((tm, d), lambda i, j: (i, 0)),
        scratch_shapes=[pltpu.VMEM((tm, d), F32)],
        compiler_params=pltpu.CompilerParams(
            dimension_semantics=("arbitrary", "arbitrary"), vmem_limit_bytes=vmem),
        name="out_proj",
    )(mix_in, w_bf16, x2, gain)


def _mlp_kernel(x_ref, gpre_ref, wu_ref, wd_ref, gpost_ref, o_ref, h_sc, a_sc, *, n_f, tn):
    s = pl.program_id(1)
    cur = s % 2

    def up(slot):
        a = jnp.dot(h_sc[...], wu_ref[...], preferred_element_type=F32)
        a_sc[slot] = jnp.square(jnp.maximum(a, 0.0)).astype(BF16)

    def down(slot):
        a = a_sc[slot]
        for k in range(o_ref.shape[1] // tn):
            cs = slice(k * tn, (k + 1) * tn)
            o_ref[:, cs] += jnp.dot(a, wd_ref[:, cs], preferred_element_type=F32)

    @pl.when(s == 0)
    def _():
        h_sc[...] = _rms_scale(x_ref[...], gpre_ref[...]).astype(BF16)
        o_ref[...] = jnp.zeros(o_ref.shape, F32)
        up(cur)

    @pl.when(jnp.logical_and(s > 0, s < n_f))
    def _():
        down(1 - cur)
        up(cur)

    @pl.when(s == n_f)
    def _():
        down(1 - cur)
        o_ref[...] = x_ref[...] + _rms_scale(o_ref[...], gpost_ref[...])


def _mlp(x1, g_pre, wu, wd, g_post, *, tm=512, tf=512, tn=1024):
    t, d = x1.shape
    d_ff = wu.shape[1]
    n_f = d_ff // tf
    vmem = (tm * d * 4 + 2 * tm * d * 4 + tm * d * 2 + 2 * 2 * d * tf * 2 + 2 * tm * tf * 2
            + 3 * tm * tf * 4 + 2 * tm * tn * 4 + tm * d * 4)
    return pl.pallas_call(
        functools.partial(_mlp_kernel, n_f=n_f, tn=tn),
        out_shape=jax.ShapeDtypeStruct((t, d), F32),
        grid=(t // tm, n_f + 1),
        in_specs=[
            pl.BlockSpec((tm, d), lambda i, s: (i, 0), pipeline_mode=pl.Buffered(1)),
            pl.BlockSpec((1, d), lambda i, s: (0, 0)),
            pl.BlockSpec((d, tf), lambda i, s: (0, jnp.minimum(s, n_f - 1))),
            pl.BlockSpec((tf, d), lambda i, s: (jnp.maximum(s - 1, 0), 0)),
            pl.BlockSpec((1, d), lambda i, s: (0, 0)),
        ],
        out_specs=pl.BlockSpec((tm, d), lambda i, s: (i, 0)),
        scratch_shapes=[pltpu.VMEM((tm, d), BF16), pltpu.VMEM((2, tm, tf), BF16)],
        compiler_params=pltpu.CompilerParams(
            dimension_semantics=("arbitrary", "arbitrary"), vmem_limit_bytes=vmem),
        name="mlp",
    )(x1, g_pre, wu, wd, g_post)


def kernel(x, norm_mix_pre, w_in, v_norm_g, v_norm_b, w_spatial, b_spatial, w_proj_a, lam_re, lam_im,
           log_dt, b_re, b_im, c_re, c_im, d_skip, w_glu_a, w_glu_b, w_out, norm_mix_post, norm_mlp_pre,
           w_ff_up, w_ff_down, norm_mlp_post):
    n_batch, seq, d_model = x.shape
    depth = w_in.shape[0]
    a_width = w_proj_a.shape[1]
    b_width = w_glu_a.shape[1]
    n_heads = w_spatial.shape[1]
    head_dim = a_width // n_heads
    x2 = x.reshape(n_batch * seq, d_model)
    for l in range(depth):
        proj, x_b, (wu, wo, wpa, wga, wgb) = _in_proj(
            x2, norm_mix_pre[l][None], w_in[l].astype(BF16),
            (w_ff_up[l], w_out[l], w_proj_a[l], w_glu_a[l], w_glu_b[l]),
            gelu_width=2 * a_width, lin_width=b_width)
        tables = _s5_tables(lam_re[l], lam_im[l], log_dt[l], b_re[l], b_im[l], c_re[l], c_im[l], d_skip[l])
        z_b = _s5_branch(x_b, tables, n_batch=n_batch, seq=seq)
        b_sp_full = jnp.repeat(b_spatial[l].T, head_dim, axis=1)
        mix_in, wd = _branches(proj, z_b, v_norm_g[l][None], v_norm_b[l][None], w_spatial[l].astype(BF16),
                               b_sp_full, wpa, wga, wgb, w_ff_down[l],
                               a_width=a_width, b_width=b_width, d_model=d_model)
        x1 = _out_proj(mix_in, wo, x2, norm_mix_post[l][None])
        x2 = _mlp(x1, norm_mlp_pre[l][None], wu, wd, norm_mlp_post[l][None])
    return x2.reshape(n_batch, seq, d_model)
```

```python
import functools

import jax
import jax.numpy as jnp
from jax import lax
from jax.experimental import pallas as pl
from jax.experimental.pallas import tpu as pltpu

F32 = jnp.float32
BF16 = jnp.bfloat16
NORM_EPS = 1e-6
HIGHEST = lax.Precision.HIGHEST

LANES = 128
A_CHUNK = 128
S5_CHUNK = 16
S5_PAIR = 2
S5_ROWS = 8


def _rms_scale(x, gain):
    ms = jnp.mean(x * x, axis=-1, keepdims=True)
    return x * lax.rsqrt(ms + NORM_EPS) * gain


def _sigmoid(x):
    return 0.5 * jnp.tanh(0.5 * x) + 0.5


def _cast_rider(w, n_i, n_j):
    r, c = w.shape
    blk = (r // n_i, c // n_j)
    assert r % n_i == 0 and c % n_j == 0 and blk[0] % 16 == 0 and blk[1] % LANES == 0

    def index(i, j):
        return i, jnp.minimum(j, n_j - 1)

    return pl.BlockSpec(blk, index), pl.BlockSpec(blk, index), jax.ShapeDtypeStruct(w.shape, BF16)


def _rider_bytes(w, n_i, n_j):
    return 2 * (w.size // (n_i * n_j)) * (4 + 2)


def _in_proj_kernel(*refs, gelu_tiles, n_riders):
    x_ref, g_ref, w_ref = refs[:3]
    cast_in = refs[3:3 + n_riders]
    o_ref, xb_ref = refs[3 + n_riders:5 + n_riders]
    cast_out = refs[5 + n_riders:5 + 2 * n_riders]
    h_sc, lin_sc = refs[5 + 2 * n_riders:]
    j = pl.program_id(1)

    @pl.when(j == 0)
    def _():
        h_sc[...] = _rms_scale(x_ref[...], g_ref[...]).astype(BF16)

    for src, dst in zip(cast_in, cast_out):
        dst[...] = src[...].astype(BF16)

    acc = jnp.dot(h_sc[...], w_ref[...], preferred_element_type=F32)

    @pl.when(j < gelu_tiles)
    def _():
        o_ref[...] = jax.nn.gelu(acc).astype(BF16)

    @pl.when(j == gelu_tiles)
    def _():
        o_ref[...] = acc.astype(BF16)
        lc, n_c = xb_ref.shape[0], xb_ref.shape[1]
        for k in range(lin_sc.shape[0]):
            cs = slice(k * LANES, (k + 1) * LANES)
            lin_sc[k] = acc[:, cs]
            for tau in range(lc):
                xb_ref[tau, :, cs] = lin_sc[k, pl.ds(tau, n_c, stride=lc), :]

    @pl.when(j > gelu_tiles)
    def _():
        o_ref[...] = _sigmoid(acc).astype(BF16)


def _in_proj(x2, gain, w_bf16, riders, *, gelu_width, lin_width, tm=512, tn=1024, rider_cols=8):
    t, d = x2.shape
    n = w_bf16.shape[1]
    n_tiles = n // tn
    assert gelu_width % tn == 0 and lin_width == tn and t % tm == 0 and n % tn == 0 and n_tiles >= rider_cols
    assert tm % (8 * S5_CHUNK) == 0
    n_i = t // tm
    rider_specs = [_cast_rider(w, n_i, rider_cols) for w in riders]
    vmem = (2 * tm * d * 4 + tm * d * 2 + 2 * d * tn * 2 + 2 * tm * tn * 2 + 3 * tm * tn * 4
            + 8 * tm * tn * 4 + sum(_rider_bytes(w, n_i, rider_cols) for w in riders))
    outs = pl.pallas_call(
        functools.partial(_in_proj_kernel, gelu_tiles=gelu_width // tn, n_riders=len(riders)),
        out_shape=(jax.ShapeDtypeStruct((t, n), BF16),
                   jax.ShapeDtypeStruct((S5_CHUNK, t // S5_CHUNK, lin_width), F32),
                   *[s[2] for s in rider_specs]),
        grid=(n_i, n_tiles),
        in_specs=[
            pl.BlockSpec((tm, d), lambda i, j: (i, 0)),
            pl.BlockSpec((1, d), lambda i, j: (0, 0)),
            pl.BlockSpec((d, tn), lambda i, j: (0, j)),
            *[s[0] for s in rider_specs],
        ],
        out_specs=(pl.BlockSpec((tm, tn), lambda i, j: (i, j)),
                   pl.BlockSpec((S5_CHUNK, tm // S5_CHUNK, lin_width), lambda i, j: (0, i, 0)),
                   *[s[1] for s in rider_specs]),
        scratch_shapes=[pltpu.VMEM((tm, d), BF16), pltpu.VMEM((lin_width // LANES, tm, LANES), F32)],
        compiler_params=pltpu.CompilerParams(
            dimension_semantics=("arbitrary", "arbitrary"), vmem_limit_bytes=vmem),
        name="in_proj",
    )(x2, gain, w_bf16, *riders)
    return outs[0], outs[1], outs[2:]


def _cmul(a, b):
    return a[0] * b[0] - a[1] * b[1], a[0] * b[1] + a[1] * b[0]


def _s5_tables(lam_re, lam_im, log_dt, b_re, b_im, c_re, c_im, d_skip):
    g, p = lam_re.shape
    hg = b_re.shape[-1]
    lc = S5_CHUNK
    npair = g // S5_PAIR
    dt = jnp.exp(log_dt)[:, None]
    mag = jnp.exp(lam_re * dt)
    a_bar = (mag * jnp.cos(lam_im * dt), mag * jnp.sin(lam_im * dt))
    den = lam_re * lam_re + lam_im * lam_im
    gain = (((a_bar[0] - 1.0) * lam_re + a_bar[1] * lam_im) / den,
            (a_bar[1] * lam_re - (a_bar[0] - 1.0) * lam_im) / den)
    b_bar = _cmul((gain[0][..., None], gain[1][..., None]), (b_re, b_im))
    pows = [(jnp.ones_like(mag), jnp.zeros_like(mag))]
    for _ in range(lc):
        pows.append(_cmul(pows[-1], a_bar))
    apow = (jnp.stack([q[0] for q in pows]), jnp.stack([q[1] for q in pows]))
    lanes = S5_PAIR * p
    own = (jnp.arange(S5_PAIR)[:, None] == (jnp.arange(lanes) // p)[None, :]).astype(F32)

    def pair_lanes(v):
        return jnp.transpose(v.reshape(v.shape[0], npair, lanes), (1, 0, 2))

    ap = (pair_lanes(apow[0]), pair_lanes(apow[1]))
    bbt = [jnp.transpose(v.reshape(npair, S5_PAIR, p, hg), (0, 3, 1, 2)).reshape(npair, 1, hg, lanes)
           * own[None, :, None, :] for v in b_bar]
    e = _cmul((ap[0][:, lc - 1::-1][:, :, None, None, :], ap[1][:, lc - 1::-1][:, :, None, None, :]),
              (bbt[0][:, None], bbt[1][:, None]))
    w1_e = jnp.concatenate(e, axis=-1).reshape(npair, lc * S5_PAIR * hg, 2 * lanes)
    cm = [jnp.transpose(v.reshape(npair, S5_PAIR, hg, p), (0, 1, 3, 2))[:, :, :, None, :]
          * jnp.eye(S5_PAIR, dtype=F32)[None, :, None, :, None] for v in (c_re, -c_im)]
    cm = jnp.concatenate([v.reshape(npair, lanes, S5_PAIR * hg) for v in cm], axis=1)
    kr = jnp.einsum('pxl,plo->pxo', w1_e, cm, precision=HIGHEST)
    n_blk = S5_PAIR * hg
    skip = d_skip.reshape(npair, n_blk)[:, :, None] * jnp.eye(n_blk, dtype=F32)[None]
    kr = jnp.concatenate([kr[:, :(lc - 1) * n_blk], kr[:, (lc - 1) * n_blk:] + skip], axis=1)
    kr = jnp.pad(kr, ((0, 0), (0, (lc - 1) * n_blk), (0, 0)))
    w1_y = jnp.concatenate(
        [kr[:, (lc - 1 - t) * n_blk:(2 * lc - 1 - t) * n_blk] for t in range(lc)], axis=-1)
    w1_y, w1_e = w1_y.astype(BF16), w1_e.astype(BF16)
    ct = [jnp.tile(v.reshape(npair, S5_PAIR, hg, p), (1, 1, 1, S5_PAIR)) * own[None, :, None, :]
          for v in (c_re, c_im)]
    cin = _cmul((ct[0][:, None], ct[1][:, None]),
                (ap[0][:, 1:lc + 1][:, :, None, None, :], ap[1][:, 1:lc + 1][:, :, None, None, :]))
    cin = jnp.concatenate([cin[0], -cin[1]], axis=-1).reshape(npair, lc * n_blk, 2 * lanes)
    cin = jnp.swapaxes(cin.astype(BF16), 1, 2)
    a_lc = (apow[0][lc], apow[1][lc])
    cpows = [a_lc]
    for _ in range(S5_ROWS - 1):
        cpows.append(_cmul(cpows[-1], a_lc))
    a_tab = jnp.stack([jnp.stack([q[0] for q in cpows]), jnp.stack([q[1] for q in cpows])])
    a_tab = jnp.transpose(a_tab.reshape(2, S5_ROWS, npair, S5_PAIR * p), (2, 0, 1, 3))
    return w1_y, w1_e, cin, a_tab


def _s5_kernel(x_ref, w1y_ref, w1e_ref, cin_ref, a_ref, z_ref, y_sc, e_sc, sp_sc, *, n_chunks, n_pair,
               pair_ch):
    lc = S5_CHUNK

    for pp in range(n_pair):
        lanes = slice(pp * pair_ch, (pp + 1) * pair_ch)
        u = jnp.concatenate([x_ref[tau][:, lanes] for tau in range(lc)], axis=1).astype(BF16)
        y_sc[pp] = jnp.dot(u, w1y_ref[pp], preferred_element_type=F32)
        e_sc[pp] = jnp.dot(u, w1e_ref[pp], preferred_element_type=F32)

    rid = lax.broadcasted_iota(jnp.int32, (S5_ROWS, LANES), 0)

    def shift_rows(v, k, fill):
        return jnp.where(rid >= k, pltpu.roll(v, k, 0), fill)

    def body(i, carry):
        new = []
        r0 = pl.multiple_of(i * S5_ROWS, S5_ROWS)
        for pp in range(n_pair):
            c_re, c_im = carry[2 * pp], carry[2 * pp + 1]
            p_re, p_im = a_ref[pp, 0], a_ref[pp, 1]
            x_re = e_sc[pp, pl.ds(r0, S5_ROWS), 0:LANES]
            x_im = e_sc[pp, pl.ds(r0, S5_ROWS), LANES:2 * LANES]
            k = 1
            while k < S5_ROWS:
                k_re, k_im = p_re[k - 1:k, :], p_im[k - 1:k, :]
                sh_re, sh_im = shift_rows(x_re, k, 0.0), shift_rows(x_im, k, 0.0)
                x_re, x_im = x_re + k_re * sh_re - k_im * sh_im, x_im + k_re * sh_im + k_im * sh_re
                k *= 2
            s_re = x_re + p_re * c_re - p_im * c_im
            s_im = x_im + p_re * c_im + p_im * c_re
            sp_sc[pp, pl.ds(r0, S5_ROWS), 0:LANES] = shift_rows(s_re, 1, c_re)
            sp_sc[pp, pl.ds(r0, S5_ROWS), LANES:2 * LANES] = shift_rows(s_im, 1, c_im)
            new.append(s_re[S5_ROWS - 1:S5_ROWS, :])
            new.append(s_im[S5_ROWS - 1:S5_ROWS, :])
        return tuple(new)

    init = tuple(jnp.zeros((1, LANES), F32) for _ in range(2 * n_pair))
    lax.fori_loop(0, n_chunks // S5_ROWS, body, init)

    ys = [jax.nn.gelu(y_sc[pp] + jnp.dot(sp_sc[pp].astype(BF16), cin_ref[pp], preferred_element_type=F32))
          for pp in range(n_pair)]
    for t in range(lc):
        z_t = jnp.concatenate([y[:, t * pair_ch:(t + 1) * pair_ch] for y in ys], axis=1)
        z_ref[pl.ds(t, n_chunks, stride=lc), :] = z_t


def _s5_branch(x_b, tables, *, n_batch, seq):
    w1_y, w1_e, cin, a_tab = tables
    npair, kw, _ = w1_y.shape
    lc, n_rows, bw = x_b.shape
    t = lc * n_rows
    n_chunks = seq // lc
    pair_ch = kw // lc
    n_pair = LANES // pair_ch
    assert lc == S5_CHUNK and n_chunks % S5_ROWS == 0 and bw % LANES == 0 and n_rows == n_batch * n_chunks
    return pl.pallas_call(
        functools.partial(_s5_kernel, n_chunks=n_chunks, n_pair=n_pair, pair_ch=pair_ch),
        out_shape=jax.ShapeDtypeStruct((t, bw), F32),
        grid=(bw // LANES, n_batch),
        in_specs=[
            pl.BlockSpec((lc, n_chunks, LANES), lambda c, b: (0, b, c)),
            pl.BlockSpec((n_pair, kw, kw), lambda c, b: (c, 0, 0)),
            pl.BlockSpec((n_pair, kw, 2 * LANES), lambda c, b: (c, 0, 0)),
            pl.BlockSpec((n_pair, 2 * LANES, kw), lambda c, b: (c, 0, 0)),
            pl.BlockSpec((n_pair, 2, S5_ROWS, LANES), lambda c, b: (c, 0, 0, 0)),
        ],
        out_specs=pl.BlockSpec((seq, LANES), lambda c, b: (b, c)),
        scratch_shapes=[pltpu.VMEM((n_pair, n_chunks, kw), F32),
                        pltpu.VMEM((n_pair, n_chunks, 2 * LANES), F32),
                        pltpu.VMEM((n_pair, n_chunks, 2 * LANES), F32)],
        compiler_params=pltpu.CompilerParams(dimension_semantics=("arbitrary", "arbitrary")),
        name="s5",
    )(x_b, w1_y, w1_e, cin, a_tab)


def _branches_kernel(u_ref, v_ref, ga_ref, gb_ref, zb_ref, lng_ref, lnb_ref, wsp_ref, bsp_ref,
                     wpa_ref, wga_ref, wgb_ref, cast_in_ref, o_ref, cast_out_ref, s_sc, *, n_heads, head_dim):
    j = pl.program_id(1)
    cast_out_ref[...] = cast_in_ref[...].astype(BF16)

    @pl.when(j == 0)
    def _():
        v = v_ref[...].astype(F32)
        mu = jnp.mean(v, axis=-1, keepdims=True)
        vc = v - mu
        var = jnp.mean(vc * vc, axis=-1, keepdims=True)
        vn = (vc * lax.rsqrt(var + NORM_EPS) * lng_ref[...] + lnb_ref[...]).astype(BF16)
        n_c = v.shape[0] // A_CHUNK
        row = lax.broadcasted_iota(jnp.int32, (A_CHUNK, A_CHUNK), 0)
        col = lax.broadcasted_iota(jnp.int32, (A_CHUNK, A_CHUNK), 1)
        causal = row >= col
        for h in range(n_heads):
            hs = slice(h * head_dim, (h + 1) * head_dim)
            w = jnp.where(causal, wsp_ref[h], jnp.zeros((), BF16))
            rhs = jnp.concatenate(
                [vn[c * A_CHUNK:(c + 1) * A_CHUNK, hs] for c in range(n_c)], axis=1)
            mixed = jnp.dot(w, rhs, preferred_element_type=F32)
            bias = bsp_ref[:, hs]
            for c in range(n_c):
                rs = slice(c * A_CHUNK, (c + 1) * A_CHUNK)
                u_blk = u_ref[rs, hs].astype(F32)
                s_sc[rs, hs] = (u_blk * (mixed[:, c * head_dim:(c + 1) * head_dim] + bias)).astype(BF16)

    zb = zb_ref[...].astype(BF16)
    br_a = jnp.dot(s_sc[...], wpa_ref[...], preferred_element_type=F32)
    glu_a = jnp.dot(zb, wga_ref[...], preferred_element_type=F32)
    glu_b = jnp.dot(zb, wgb_ref[...], preferred_element_type=F32)
    mix = ga_ref[...].astype(F32) * br_a + gb_ref[...].astype(F32) * (glu_a * _sigmoid(glu_b))
    o_ref[...] = mix.astype(BF16)


def _branches(proj, z_b, ln_g, ln_b, w_sp, b_sp_full, wpa, wga, wgb, rider, *, a_width, b_width, d_model,
              tm=512, tn=1024):
    t = proj.shape[0]
    n_heads = w_sp.shape[0]
    head_dim = a_width // n_heads
    ga_off = (2 * a_width + b_width) // tn
    gb_off = (2 * a_width + b_width + d_model) // tn
    assert (2 * a_width + b_width) % tn == 0 and d_model % tn == 0 and t % tm == 0 and tm % A_CHUNK == 0
    n_i, n_j = t // tm, d_model // tn
    rider_in, rider_out, rider_shape = _cast_rider(rider, n_i, n_j)
    vmem = (2 * 2 * tm * a_width * 2 + 2 * 2 * tm * tn * 2 + 2 * tm * b_width * 4
            + 2 * a_width * tn * 2 + 2 * 2 * b_width * tn * 2 + tm * a_width * 2 + 2 * tm * tn * 2
            + 4 * tm * a_width * 4 + 6 * tm * tn * 4 + _rider_bytes(rider, n_i, n_j))
    return pl.pallas_call(
        functools.partial(_branches_kernel, n_heads=n_heads, head_dim=head_dim),
        out_shape=(jax.ShapeDtypeStruct((t, d_model), BF16), rider_shape),
        grid=(n_i, n_j),
        in_specs=[
            pl.BlockSpec((tm, a_width), lambda i, j: (i, 0)),
            pl.BlockSpec((tm, a_width), lambda i, j: (i, 1)),
            pl.BlockSpec((tm, tn), lambda i, j: (i, ga_off + j)),
            pl.BlockSpec((tm, tn), lambda i, j: (i, gb_off + j)),
            pl.BlockSpec((tm, b_width), lambda i, j: (i, 0)),
            pl.BlockSpec((1, a_width), lambda i, j: (0, 0)),
            pl.BlockSpec((1, a_width), lambda i, j: (0, 0)),
            pl.BlockSpec((n_heads, A_CHUNK, A_CHUNK), lambda i, j: (0, 0, 0)),
            pl.BlockSpec((A_CHUNK, a_width), lambda i, j: (0, 0)),
            pl.BlockSpec((a_width, tn), lambda i, j: (0, j)),
            pl.BlockSpec((b_width, tn), lambda i, j: (0, j)),
            pl.BlockSpec((b_width, tn), lambda i, j: (0, j)),
            rider_in,
        ],
        out_specs=(pl.BlockSpec((tm, tn), lambda i, j: (i, j)), rider_out),
        scratch_shapes=[pltpu.VMEM((tm, a_width), BF16)],
        compiler_params=pltpu.CompilerParams(
            dimension_semantics=("arbitrary", "arbitrary"), vmem_limit_bytes=vmem),
        name="branches",
    )(proj, proj, proj, proj, z_b, ln_g, ln_b, w_sp, b_sp_full, wpa, wga, wgb, rider)


def _out_proj_kernel(m_ref, w_ref, x_ref, g_ref, o_ref, x_sc, *, n_tiles, tn):
    j = pl.program_id(1)
    for k in range(n_tiles):
        @pl.when(j == k)
        def _(k=k):
            cs = slice(k * tn, (k + 1) * tn)
            o_ref[:, cs] = jnp.dot(m_ref[...], w_ref[...], preferred_element_type=F32)
            x_sc[:, cs] = x_ref[...]

    @pl.when(j == n_tiles - 1)
    def _():
        ssq = jnp.zeros((o_ref.shape[0], 1), F32)
        for k in range(n_tiles):
            a = o_ref[:, k * tn:(k + 1) * tn]
            ssq = ssq + jnp.sum(a * a, axis=-1, keepdims=True)
        inv = lax.rsqrt(ssq / (n_tiles * tn) + NORM_EPS)
        for k in range(n_tiles):
            cs = slice(k * tn, (k + 1) * tn)
            o_ref[:, cs] = x_sc[:, cs] + o_ref[:, cs] * inv * g_ref[:, cs]


def _out_proj(mix_in, w_bf16, x2, gain, *, tm=512, tn=1024):
    t, d = x2.shape
    n_tiles = d // tn
    vmem = (2 * tm * d * 2 + 2 * d * tn * 2 + 2 * tm * tn * 4 + 2 * tm * d * 4 + tm * d * 4 + 4 * tm * tn * 4)
    return pl.pallas_call(
        functools.partial(_out_proj_kernel, n_tiles=n_tiles, tn=tn),
        out_shape=jax.ShapeDtypeStruct((t, d), F32),
        grid=(t // tm, n_tiles),
        in_specs=[
            pl.BlockSpec((tm, d), lambda i, j: (i, 0)),
            pl.BlockSpec((d, tn), lambda i, j: (0, j)),
            pl.BlockSpec((tm, tn), lambda i, j: (i, j)),
            pl.BlockSpec((1, d), lambda i, j: (0, 0)),
        ],
        out_specs=pl.BlockSpec((tm, d), lambda i, j: (i, 0)),
        scratch_shapes=[pltpu.VMEM((tm, d), F32)],
        compiler_params=pltpu.CompilerParams(
            dimension_semantics=("arbitrary", "arbitrary"), vmem_limit_bytes=vmem),
        name="out_proj",
    )(mix_in, w_bf16, x2, gain)


def _mlp_kernel(x_ref, gpre_ref, wu_ref, wd_ref, gpost_ref, o_ref, h_sc, a_sc, *, n_f, tn):
    s = pl.program_id(1)
    cur = s % 2

    def up(slot):
        a = jnp.dot(h_sc[...], wu_ref[...], preferred_element_type=F32)
        a_sc[slot] = jnp.square(jnp.maximum(a, 0.0)).astype(BF16)

    def down(slot):
        a = a_sc[slot]
        for k in range(o_ref.shape[1] // tn):
            cs = slice(k * tn, (k + 1) * tn)
            o_ref[:, cs] += jnp.dot(a, wd_ref[:, cs], preferred_element_type=F32)

    @pl.when(s == 0)
    def _():
        h_sc[...] = _rms_scale(x_ref[...], gpre_ref[...]).astype(BF16)
        o_ref[...] = jnp.zeros(o_ref.shape, F32)
        up(cur)

    @pl.when(jnp.logical_and(s > 0, s < n_f))
    def _():
        down(1 - cur)
        up(cur)

    @pl.when(s == n_f)
    def _():
        down(1 - cur)
        o_ref[...] = x_ref[...] + _rms_scale(o_ref[...], gpost_ref[...])


def _mlp(x1, g_pre, wu, wd, g_post, *, tm=512, tf=512, tn=1024):
    t, d = x1.shape
    d_ff = wu.shape[1]
    n_f = d_ff // tf
    vmem = (2 * tm * d * 4 + 2 * tm * d * 4 + tm * d * 2 + 2 * 2 * d * tf * 2 + 2 * tm * tf * 2
            + 3 * tm * tf * 4 + 2 * tm * tn * 4 + tm * d * 4)
    return pl.pallas_call(
        functools.partial(_mlp_kernel, n_f=n_f, tn=tn),
        out_shape=jax.ShapeDtypeStruct((t, d), F32),
        grid=(t // tm, n_f + 1),
        in_specs=[
            pl.BlockSpec((tm, d), lambda i, s: (i, 0)),
            pl.BlockSpec((1, d), lambda i, s: (0, 0)),
            pl.BlockSpec((d, tf), lambda i, s: (0, jnp.minimum(s, n_f - 1))),
            pl.BlockSpec((tf, d), lambda i, s: (jnp.maximum(s - 1, 0), 0)),
            pl.BlockSpec((1, d), lambda i, s: (0, 0)),
        ],
        out_specs=pl.BlockSpec((tm, d), lambda i, s: (i, 0)),
        scratch_shapes=[pltpu.VMEM((tm, d), BF16), pltpu.VMEM((2, tm, tf), BF16)],
        compiler_params=pltpu.CompilerParams(
            dimension_semantics=("arbitrary", "arbitrary"), vmem_limit_bytes=vmem),
        name="mlp",
    )(x1, g_pre, wu, wd, g_post)


def kernel(x, norm_mix_pre, w_in, v_norm_g, v_norm_b, w_spatial, b_spatial, w_proj_a, lam_re, lam_im,
           log_dt, b_re, b_im, c_re, c_im, d_skip, w_glu_a, w_glu_b, w_out, norm_mix_post, norm_mlp_pre,
           w_ff_up, w_ff_down, norm_mlp_post):
    n_batch, seq, d_model = x.shape
    depth = w_in.shape[0]
    a_width = w_proj_a.shape[1]
    b_width = w_glu_a.shape[1]
    n_heads = w_spatial.shape[1]
    head_dim = a_width // n_heads
    x2 = x.reshape(n_batch * seq, d_model)
    for l in range(depth):
        proj, x_b, (wu, wo, wpa, wga, wgb) = _in_proj(
            x2, norm_mix_pre[l][None], w_in[l].astype(BF16),
            (w_ff_up[l], w_out[l], w_proj_a[l], w_glu_a[l], w_glu_b[l]),
            gelu_width=2 * a_width, lin_width=b_width)
        tables = _s5_tables(lam_re[l], lam_im[l], log_dt[l], b_re[l], b_im[l], c_re[l], c_im[l], d_skip[l])
        z_b = _s5_branch(x_b, tables, n_batch=n_batch, seq=seq)
        b_sp_full = jnp.repeat(b_spatial[l].T, head_dim, axis=1)
        mix_in, wd = _branches(proj, z_b, v_norm_g[l][None], v_norm_b[l][None], w_spatial[l].astype(BF16),
                               b_sp_full, wpa, wga, wgb, w_ff_down[l],
                               a_width=a_width, b_width=b_width, d_model=d_model)
        x1 = _out_proj(mix_in, wo, x2, norm_mix_post[l][None])
        x2 = _mlp(x1, norm_mlp_pre[l][None], wu, wd, norm_mlp_post[l][None])
    return x2.reshape(n_batch, seq, d_model)
```

```python
import functools

import jax
import jax.numpy as jnp
from jax import lax
from jax.experimental import pallas as pl
from jax.experimental.pallas import tpu as pltpu

F32 = jnp.float32
BF16 = jnp.bfloat16
NORM_EPS = 1e-6
HIGHEST = lax.Precision.HIGHEST

LANES = 128
A_CHUNK = 128
S5_CHUNK = 16
S5_PAIR = 2
S5_ROWS = 8


def _rms_scale(x, gain):
    ms = jnp.mean(x * x, axis=-1, keepdims=True)
    return x * lax.rsqrt(ms + NORM_EPS) * gain


def _sigmoid(x):
    return 0.5 * jnp.tanh(0.5 * x) + 0.5


def _cast_rider(w, n_i, n_j):
    r, c = w.shape
    blk = (r // n_i, c // n_j)
    assert r % n_i == 0 and c % n_j == 0 and blk[0] % 16 == 0 and blk[1] % LANES == 0

    def index(i, j):
        return i, jnp.minimum(j, n_j - 1)

    return pl.BlockSpec(blk, index), pl.BlockSpec(blk, index), jax.ShapeDtypeStruct(w.shape, BF16)


def _rider_bytes(w, n_i, n_j):
    return 2 * (w.size // (n_i * n_j)) * (4 + 2)


def _in_proj_kernel(*refs, gelu_tiles, n_riders):
    x_ref, g_ref, w_ref = refs[:3]
    cast_in = refs[3:3 + n_riders]
    o_ref, xb_ref = refs[3 + n_riders:5 + n_riders]
    cast_out = refs[5 + n_riders:5 + 2 * n_riders]
    h_sc, lin_sc = refs[5 + 2 * n_riders:]
    j = pl.program_id(1)

    @pl.when(j == 0)
    def _():
        h_sc[...] = _rms_scale(x_ref[...], g_ref[...]).astype(BF16)

    for src, dst in zip(cast_in, cast_out):
        dst[...] = src[...].astype(BF16)

    acc = jnp.dot(h_sc[...], w_ref[...], preferred_element_type=F32)

    @pl.when(j < gelu_tiles)
    def _():
        o_ref[...] = jax.nn.gelu(acc).astype(BF16)

    @pl.when(j == gelu_tiles)
    def _():
        o_ref[...] = acc.astype(BF16)
        lc, n_c = xb_ref.shape[0], xb_ref.shape[1]
        for k in range(lin_sc.shape[0]):
            cs = slice(k * LANES, (k + 1) * LANES)
            lin_sc[k] = acc[:, cs]
            for tau in range(lc):
                xb_ref[tau, :, cs] = lin_sc[k, pl.ds(tau, n_c, stride=lc), :]

    @pl.when(j > gelu_tiles)
    def _():
        o_ref[...] = _sigmoid(acc).astype(BF16)


def _in_proj(x2, gain, w_bf16, riders, *, gelu_width, lin_width, tm=512, tn=1024, rider_cols=8):
    t, d = x2.shape
    n = w_bf16.shape[1]
    n_tiles = n // tn
    assert gelu_width % tn == 0 and lin_width == tn and t % tm == 0 and n % tn == 0 and n_tiles >= rider_cols
    assert tm % (8 * S5_CHUNK) == 0
    n_i = t // tm
    rider_specs = [_cast_rider(w, n_i, rider_cols) for w in riders]
    vmem = (2 * tm * d * 4 + tm * d * 2 + 2 * d * tn * 2 + 2 * tm * tn * 2 + 3 * tm * tn * 4
            + 8 * tm * tn * 4 + sum(_rider_bytes(w, n_i, rider_cols) for w in riders))
    outs = pl.pallas_call(
        functools.partial(_in_proj_kernel, gelu_tiles=gelu_width // tn, n_riders=len(riders)),
        out_shape=(jax.ShapeDtypeStruct((t, n), BF16),
                   jax.ShapeDtypeStruct((S5_CHUNK, t // S5_CHUNK, lin_width), F32),
                   *[s[2] for s in rider_specs]),
        grid=(n_i, n_tiles),
        in_specs=[
            pl.BlockSpec((tm, d), lambda i, j: (i, 0)),
            pl.BlockSpec((1, d), lambda i, j: (0, 0)),
            pl.BlockSpec((d, tn), lambda i, j: (0, j)),
            *[s[0] for s in rider_specs],
        ],
        out_specs=(pl.BlockSpec((tm, tn), lambda i, j: (i, j)),
                   pl.BlockSpec((S5_CHUNK, tm // S5_CHUNK, lin_width), lambda i, j: (0, i, 0)),
                   *[s[1] for s in rider_specs]),
        scratch_shapes=[pltpu.VMEM((tm, d), BF16), pltpu.VMEM((lin_width // LANES, tm, LANES), F32)],
        compiler_params=pltpu.CompilerParams(
            dimension_semantics=("arbitrary", "arbitrary"), vmem_limit_bytes=vmem),
        name="in_proj",
    )(x2, gain, w_bf16, *riders)
    return outs[0], outs[1], outs[2:]


def _cmul(a, b):
    return a[0] * b[0] - a[1] * b[1], a[0] * b[1] + a[1] * b[0]


def _s5_tables(lam_re, lam_im, log_dt, b_re, b_im, c_re, c_im, d_skip):
    g, p = lam_re.shape
    hg = b_re.shape[-1]
    lc = S5_CHUNK
    npair = g // S5_PAIR
    dt = jnp.exp(log_dt)[:, None]
    mag = jnp.exp(lam_re * dt)
    a_bar = (mag * jnp.cos(lam_im * dt), mag * jnp.sin(lam_im * dt))
    den = lam_re * lam_re + lam_im * lam_im
    gain = (((a_bar[0] - 1.0) * lam_re + a_bar[1] * lam_im) / den,
            (a_bar[1] * lam_re - (a_bar[0] - 1.0) * lam_im) / den)
    b_bar = _cmul((gain[0][..., None], gain[1][..., None]), (b_re, b_im))
    pows = [(jnp.ones_like(mag), jnp.zeros_like(mag))]
    for _ in range(lc):
        pows.append(_cmul(pows[-1], a_bar))
    apow = (jnp.stack([q[0] for q in pows]), jnp.stack([q[1] for q in pows]))
    lanes = S5_PAIR * p
    own = (jnp.arange(S5_PAIR)[:, None] == (jnp.arange(lanes) // p)[None, :]).astype(F32)

    def pair_lanes(v):
        return jnp.transpose(v.reshape(v.shape[0], npair, lanes), (1, 0, 2))

    ap = (pair_lanes(apow[0]), pair_lanes(apow[1]))
    bbt = [jnp.transpose(v.reshape(npair, S5_PAIR, p, hg), (0, 3, 1, 2)).reshape(npair, 1, hg, lanes)
           * own[None, :, None, :] for v in b_bar]
    e = _cmul((ap[0][:, lc - 1::-1][:, :, None, None, :], ap[1][:, lc - 1::-1][:, :, None, None, :]),
              (bbt[0][:, None], bbt[1][:, None]))
    w1_e = jnp.concatenate(e, axis=-1).reshape(npair, lc * S5_PAIR * hg, 2 * lanes)
    cm = [jnp.transpose(v.reshape(npair, S5_PAIR, hg, p), (0, 1, 3, 2))[:, :, :, None, :]
          * jnp.eye(S5_PAIR, dtype=F32)[None, :, None, :, None] for v in (c_re, -c_im)]
    cm = jnp.concatenate([v.reshape(npair, lanes, S5_PAIR * hg) for v in cm], axis=1)
    kr = jnp.einsum('pxl,plo->pxo', w1_e, cm, precision=HIGHEST)
    n_blk = S5_PAIR * hg
    skip = d_skip.reshape(npair, n_blk)[:, :, None] * jnp.eye(n_blk, dtype=F32)[None]
    kr = jnp.concatenate([kr[:, :(lc - 1) * n_blk], kr[:, (lc - 1) * n_blk:] + skip], axis=1).astype(BF16)
    kr = jnp.pad(kr, ((0, 0), (0, (lc - 1) * n_blk), (0, 0)))
    w1_y = jnp.concatenate(
        [kr[:, (lc - 1 - t) * n_blk:(2 * lc - 1 - t) * n_blk] for t in range(lc)], axis=-1)
    w1_e = w1_e.astype(BF16)
    ct = [jnp.tile(v.reshape(npair, S5_PAIR, hg, p), (1, 1, 1, S5_PAIR)) * own[None, :, None, :]
          for v in (c_re, c_im)]
    cin = _cmul((ct[0][:, None], ct[1][:, None]),
                (ap[0][:, 1:lc + 1][:, :, None, None, :], ap[1][:, 1:lc + 1][:, :, None, None, :]))
    cin = jnp.concatenate([cin[0], -cin[1]], axis=-1).reshape(npair, lc * n_blk, 2 * lanes)
    cin = jnp.swapaxes(cin.astype(BF16), 1, 2)
    a_lc = (apow[0][lc], apow[1][lc])
    cpows = [a_lc]
    for _ in range(S5_ROWS - 1):
        cpows.append(_cmul(cpows[-1], a_lc))
    a_tab = jnp.stack([jnp.stack([q[0] for q in cpows]), jnp.stack([q[1] for q in cpows])])
    a_tab = jnp.transpose(a_tab.reshape(2, S5_ROWS, npair, S5_PAIR * p), (2, 0, 1, 3))
    return w1_y, w1_e, cin, a_tab


def _s5_kernel(x_ref, w1y_ref, w1e_ref, cin_ref, a_ref, z_ref, y_sc, e_sc, sp_sc, *, n_chunks, n_pair,
               pair_ch):
    lc = S5_CHUNK

    for pp in range(n_pair):
        lanes = slice(pp * pair_ch, (pp + 1) * pair_ch)
        u = jnp.concatenate([x_ref[tau][:, lanes] for tau in range(lc)], axis=1).astype(BF16)
        y_sc[pp] = jnp.dot(u, w1y_ref[pp], preferred_element_type=F32)
        e_sc[pp] = jnp.dot(u, w1e_ref[pp], preferred_element_type=F32)

    rid = lax.broadcasted_iota(jnp.int32, (S5_ROWS, LANES), 0)

    def shift_rows(v, k, fill):
        return jnp.where(rid >= k, pltpu.roll(v, k, 0), fill)

    def body(i, carry):
        new = []
        r0 = pl.multiple_of(i * S5_ROWS, S5_ROWS)
        for pp in range(n_pair):
            c_re, c_im = carry[2 * pp], carry[2 * pp + 1]
            p_re, p_im = a_ref[pp, 0], a_ref[pp, 1]
            x_re = e_sc[pp, pl.ds(r0, S5_ROWS), 0:LANES]
            x_im = e_sc[pp, pl.ds(r0, S5_ROWS), LANES:2 * LANES]
            k = 1
            while k < S5_ROWS:
                k_re, k_im = p_re[k - 1:k, :], p_im[k - 1:k, :]
                sh_re, sh_im = shift_rows(x_re, k, 0.0), shift_rows(x_im, k, 0.0)
                x_re, x_im = x_re + k_re * sh_re - k_im * sh_im, x_im + k_re * sh_im + k_im * sh_re
                k *= 2
            s_re = x_re + p_re * c_re - p_im * c_im
            s_im = x_im + p_re * c_im + p_im * c_re
            sp_sc[pp, pl.ds(r0, S5_ROWS), 0:LANES] = shift_rows(s_re, 1, c_re)
            sp_sc[pp, pl.ds(r0, S5_ROWS), LANES:2 * LANES] = shift_rows(s_im, 1, c_im)
            new.append(s_re[S5_ROWS - 1:S5_ROWS, :])
            new.append(s_im[S5_ROWS - 1:S5_ROWS, :])
        return tuple(new)

    init = tuple(jnp.zeros((1, LANES), F32) for _ in range(2 * n_pair))
    lax.fori_loop(0, n_chunks // S5_ROWS, body, init)

    ys = [jax.nn.gelu(y_sc[pp] + jnp.dot(sp_sc[pp].astype(BF16), cin_ref[pp], preferred_element_type=F32))
          for pp in range(n_pair)]
    for t in range(lc):
        z_t = jnp.concatenate([y[:, t * pair_ch:(t + 1) * pair_ch] for y in ys], axis=1)
        z_ref[pl.ds(t, n_chunks, stride=lc), :] = z_t


def _s5_branch(x_b, tables, *, n_batch, seq):
    w1_y, w1_e, cin, a_tab = tables
    npair, kw, _ = w1_y.shape
    lc, n_rows, bw = x_b.shape
    t = lc * n_rows
    n_chunks = seq // lc
    pair_ch = kw // lc
    n_pair = LANES // pair_ch
    assert lc == S5_CHUNK and n_chunks % S5_ROWS == 0 and bw % LANES == 0 and n_rows == n_batch * n_chunks
    return pl.pallas_call(
        functools.partial(_s5_kernel, n_chunks=n_chunks, n_pair=n_pair, pair_ch=pair_ch),
        out_shape=jax.ShapeDtypeStruct((t, bw), F32),
        grid=(bw // LANES, n_batch),
        in_specs=[
            pl.BlockSpec((lc, n_chunks, LANES), lambda c, b: (0, b, c)),
            pl.BlockSpec((n_pair, kw, kw), lambda c, b: (c, 0, 0)),
            pl.BlockSpec((n_pair, kw, 2 * LANES), lambda c, b: (c, 0, 0)),
            pl.BlockSpec((n_pair, 2 * LANES, kw), lambda c, b: (c, 0, 0)),
            pl.BlockSpec((n_pair, 2, S5_ROWS, LANES), lambda c, b: (c, 0, 0, 0)),
        ],
        out_specs=pl.BlockSpec((seq, LANES), lambda c, b: (b, c)),
        scratch_shapes=[pltpu.VMEM((n_pair, n_chunks, kw), F32),
                        pltpu.VMEM((n_pair, n_chunks, 2 * LANES), F32),
                        pltpu.VMEM((n_pair, n_chunks, 2 * LANES), F32)],
        compiler_params=pltpu.CompilerParams(dimension_semantics=("arbitrary", "arbitrary")),
        name="s5",
    )(x_b, w1_y, w1_e, cin, a_tab)


def _branches_kernel(u_ref, v_ref, ga_ref, gb_ref, zb_ref, lng_ref, lnb_ref, wsp_ref, bsp_ref,
                     wpa_ref, wga_ref, wgb_ref, cast_in_ref, o_ref, cast_out_ref, s_sc, *, n_heads, head_dim):
    j = pl.program_id(1)
    cast_out_ref[...] = cast_in_ref[...].astype(BF16)

    def glu_branch():
        zb = zb_ref[...].astype(BF16)
        glu_a = jnp.dot(zb, wga_ref[...], preferred_element_type=F32)
        glu_b = jnp.dot(zb, wgb_ref[...], preferred_element_type=F32)
        return gb_ref[...].astype(F32) * (glu_a * _sigmoid(glu_b))

    def merge(mix_b):
        br_a = jnp.dot(s_sc[...], wpa_ref[...], preferred_element_type=F32)
        o_ref[...] = (ga_ref[...].astype(F32) * br_a + mix_b).astype(BF16)

    def spatial_gating():
        v = v_ref[...].astype(F32)
        mu = jnp.mean(v, axis=-1, keepdims=True)
        vc = v - mu
        var = jnp.mean(vc * vc, axis=-1, keepdims=True)
        vn = (vc * lax.rsqrt(var + NORM_EPS) * lng_ref[...] + lnb_ref[...]).astype(BF16)
        n_c = v.shape[0] // A_CHUNK
        row = lax.broadcasted_iota(jnp.int32, (A_CHUNK, A_CHUNK), 0)
        col = lax.broadcasted_iota(jnp.int32, (A_CHUNK, A_CHUNK), 1)
        causal = row >= col
        for h in range(n_heads):
            hs = slice(h * head_dim, (h + 1) * head_dim)
            w = jnp.where(causal, wsp_ref[h], jnp.zeros((), BF16))
            rhs = jnp.concatenate(
                [vn[c * A_CHUNK:(c + 1) * A_CHUNK, hs] for c in range(n_c)], axis=1)
            mixed = jnp.dot(w, rhs, preferred_element_type=F32)
            bias = bsp_ref[:, hs]
            for c in range(n_c):
                rs = slice(c * A_CHUNK, (c + 1) * A_CHUNK)
                u_blk = u_ref[rs, hs].astype(F32)
                s_sc[rs, hs] = (u_blk * (mixed[:, c * head_dim:(c + 1) * head_dim] + bias)).astype(BF16)

    @pl.when(j == 0)
    def _():
        mix_b = glu_branch()
        spatial_gating()
        merge(mix_b)

    @pl.when(j > 0)
    def _():
        merge(glu_branch())


def _branches(proj, z_b, ln_g, ln_b, w_sp, b_sp_full, wpa, wga, wgb, rider, *, a_width, b_width, d_model,
              tm=512, tn=1024):
    t = proj.shape[0]
    n_heads = w_sp.shape[0]
    head_dim = a_width // n_heads
    ga_off = (2 * a_width + b_width) // tn
    gb_off = (2 * a_width + b_width + d_model) // tn
    assert (2 * a_width + b_width) % tn == 0 and d_model % tn == 0 and t % tm == 0 and tm % A_CHUNK == 0
    n_i, n_j = t // tm, d_model // tn
    rider_in, rider_out, rider_shape = _cast_rider(rider, n_i, n_j)
    vmem = (2 * 2 * tm * a_width * 2 + 2 * 2 * tm * tn * 2 + 2 * tm * b_width * 4
            + 2 * a_width * tn * 2 + 2 * 2 * b_width * tn * 2 + tm * a_width * 2 + 2 * tm * tn * 2
            + 4 * tm * a_width * 4 + 6 * tm * tn * 4 + _rider_bytes(rider, n_i, n_j))
    return pl.pallas_call(
        functools.partial(_branches_kernel, n_heads=n_heads, head_dim=head_dim),
        out_shape=(jax.ShapeDtypeStruct((t, d_model), BF16), rider_shape),
        grid=(n_i, n_j),
        in_specs=[
            pl.BlockSpec((tm, a_width), lambda i, j: (i, 0)),
            pl.BlockSpec((tm, a_width), lambda i, j: (i, 1)),
            pl.BlockSpec((tm, tn), lambda i, j: (i, ga_off + j)),
            pl.BlockSpec((tm, tn), lambda i, j: (i, gb_off + j)),
            pl.BlockSpec((tm, b_width), lambda i, j: (i, 0)),
            pl.BlockSpec((1, a_width), lambda i, j: (0, 0)),
            pl.BlockSpec((1, a_width), lambda i, j: (0, 0)),
            pl.BlockSpec((n_heads, A_CHUNK, A_CHUNK), lambda i, j: (0, 0, 0)),
            pl.BlockSpec((A_CHUNK, a_width), lambda i, j: (0, 0)),
            pl.BlockSpec((a_width, tn), lambda i, j: (0, j)),
            pl.BlockSpec((b_width, tn), lambda i, j: (0, j)),
            pl.BlockSpec((b_width, tn), lambda i, j: (0, j)),
            rider_in,
        ],
        out_specs=(pl.BlockSpec((tm, tn), lambda i, j: (i, j)), rider_out),
        scratch_shapes=[pltpu.VMEM((tm, a_width), BF16)],
        compiler_params=pltpu.CompilerParams(
            dimension_semantics=("arbitrary", "arbitrary"), vmem_limit_bytes=vmem),
        name="branches",
    )(proj, proj, proj, proj, z_b, ln_g, ln_b, w_sp, b_sp_full, wpa, wga, wgb, rider)


def _out_proj_kernel(m_ref, w_ref, x_ref, g_ref, o_ref, x_sc, *, n_tiles, tn):
    j = pl.program_id(1)
    for k in range(n_tiles):
        @pl.when(j == k)
        def _(k=k):
            cs = slice(k * tn, (k + 1) * tn)
            o_ref[:, cs] = jnp.dot(m_ref[...], w_ref[...], preferred_element_type=F32)
            x_sc[:, cs] = x_ref[...]

    @pl.when(j == n_tiles - 1)
    def _():
        ssq = jnp.zeros((o_ref.shape[0], 1), F32)
        for k in range(n_tiles):
            a = o_ref[:, k * tn:(k + 1) * tn]
            ssq = ssq + jnp.sum(a * a, axis=-1, keepdims=True)
        inv = lax.rsqrt(ssq / (n_tiles * tn) + NORM_EPS)
        for k in range(n_tiles):
            cs = slice(k * tn, (k + 1) * tn)
            o_ref[:, cs] = x_sc[:, cs] + o_ref[:, cs] * inv * g_ref[:, cs]


def _out_proj(mix_in, w_bf16, x2, gain, *, tm=512, tn=1024):
    t, d = x2.shape
    n_tiles = d // tn
    vmem = (2 * tm * d * 2 + 2 * d * tn * 2 + 2 * tm * tn * 4 + 2 * tm * d * 4 + tm * d * 4 + 4 * tm * tn * 4)
    return pl.pallas_call(
        functools.partial(_out_proj_kernel, n_tiles=n_tiles, tn=tn),
        out_shape=jax.ShapeDtypeStruct((t, d), F32),
        grid=(t // tm, n_tiles),
        in_specs=[
            pl.BlockSpec((tm, d), lambda i, j: (i, 0)),
            pl.BlockSpec((d, tn), lambda i, j: (0, j)),
            pl.BlockSpec((tm, tn), lambda i, j: (i, j)),
            pl.BlockSpec((1, d), lambda i, j: (0, 0)),
        ],
        out_specs=pl.BlockSpec((tm, d), lambda i, j: (i, 0)),
        scratch_shapes=[pltpu.VMEM((tm, d), F32)],
        compiler_params=pltpu.CompilerParams(
            dimension_semantics=("arbitrary", "arbitrary"), vmem_limit_bytes=vmem),
        name="out_proj",
    )(mix_in, w_bf16, x2, gain)


def _mlp_kernel(x_ref, gpre_ref, wu_ref, wd_ref, gpost_ref, o_ref, h_sc, a_sc, *, n_f, tn):
    s = pl.program_id(1)
    cur = s % 2

    def up(slot):
        a = jnp.dot(h_sc[...], wu_ref[...], preferred_element_type=F32)
        a_sc[slot] = jnp.square(jnp.maximum(a, 0.0)).astype(BF16)

    def down(slot):
        a = a_sc[slot]
        for k in range(o_ref.shape[1] // tn):
            cs = slice(k * tn, (k + 1) * tn)
            o_ref[:, cs] += jnp.dot(a, wd_ref[:, cs], preferred_element_type=F32)

    @pl.when(s == 0)
    def _():
        h_sc[...] = _rms_scale(x_ref[...], gpre_ref[...]).astype(BF16)
        o_ref[...] = jnp.zeros(o_ref.shape, F32)
        up(cur)

    @pl.when(jnp.logical_and(s > 0, s < n_f))
    def _():
        down(1 - cur)
        up(cur)

    @pl.when(s == n_f)
    def _():
        down(1 - cur)
        o_ref[...] = x_ref[...] + _rms_scale(o_ref[...], gpost_ref[...])


def _mlp(x1, g_pre, wu, wd, g_post, *, tm=512, tf=512, tn=1024):
    t, d = x1.shape
    d_ff = wu.shape[1]
    n_f = d_ff // tf
    vmem = (2 * tm * d * 4 + 2 * tm * d * 4 + tm * d * 2 + 2 * 2 * d * tf * 2 + 2 * tm * tf * 2
            + 3 * tm * tf * 4 + 2 * tm * tn * 4 + tm * d * 4)
    return pl.pallas_call(
        functools.partial(_mlp_kernel, n_f=n_f, tn=tn),
        out_shape=jax.ShapeDtypeStruct((t, d), F32),
        grid=(t // tm, n_f + 1),
        in_specs=[
            pl.BlockSpec((tm, d), lambda i, s: (i, 0)),
            pl.BlockSpec((1, d), lambda i, s: (0, 0)),
            pl.BlockSpec((d, tf), lambda i, s: (0, jnp.minimum(s, n_f - 1))),
            pl.BlockSpec((tf, d), lambda i, s: (jnp.maximum(s - 1, 0), 0)),
            pl.BlockSpec((1, d), lambda i, s: (0, 0)),
        ],
        out_specs=pl.BlockSpec((tm, d), lambda i, s: (i, 0)),
        scratch_shapes=[pltpu.VMEM((tm, d), BF16), pltpu.VMEM((2, tm, tf), BF16)],
        compiler_params=pltpu.CompilerParams(
            dimension_semantics=("arbitrary", "arbitrary"), vmem_limit_bytes=vmem),
        name="mlp",
    )(x1, g_pre, wu, wd, g_post)


def kernel(x, norm_mix_pre, w_in, v_norm_g, v_norm_b, w_spatial, b_spatial, w_proj_a, lam_re, lam_im,
           log_dt, b_re, b_im, c_re, c_im, d_skip, w_glu_a, w_glu_b, w_out, norm_mix_post, norm_mlp_pre,
           w_ff_up, w_ff_down, norm_mlp_post):
    n_batch, seq, d_model = x.shape
    depth = w_in.shape[0]
    a_width = w_proj_a.shape[1]
    b_width = w_glu_a.shape[1]
    n_heads = w_spatial.shape[1]
    head_dim = a_width // n_heads
    x2 = x.reshape(n_batch * seq, d_model)
    for l in range(depth):
        proj, x_b, (wu, wo, wpa, wga, wgb) = _in_proj(
            x2, norm_mix_pre[l][None], w_in[l].astype(BF16),
            (w_ff_up[l], w_out[l], w_proj_a[l], w_glu_a[l], w_glu_b[l]),
            gelu_width=2 * a_width, lin_width=b_width)
        tables = _s5_tables(lam_re[l], lam_im[l], log_dt[l], b_re[l], b_im[l], c_re[l], c_im[l], d_skip[l])
        z_b = _s5_branch(x_b, tables, n_batch=n_batch, seq=seq)
        b_sp_full = jnp.repeat(b_spatial[l].T, head_dim, axis=1)
        mix_in, wd = _branches(proj, z_b, v_norm_g[l][None], v_norm_b[l][None], w_spatial[l].astype(BF16),
                               b_sp_full, wpa, wga, wgb, w_ff_down[l],
                               a_width=a_width, b_width=b_width, d_model=d_model)
        x1 = _out_proj(mix_in, wo, x2, norm_mix_post[l][None])
        x2 = _mlp(x1, norm_mlp_pre[l][None], wu, wd, norm_mlp_post[l][None])
    return x2.reshape(n_batch, seq, d_model)
```

```python
import functools

import jax
import jax.numpy as jnp
from jax import lax
from jax.experimental import pallas as pl
from jax.experimental.pallas import tpu as pltpu

F32 = jnp.float32
BF16 = jnp.bfloat16
NORM_EPS = 1e-6
HIGHEST = lax.Precision.HIGHEST

LANES = 128
A_CHUNK = 128
S5_CHUNK = 16
S5_PAIR = 2
S5_ROWS = 8


def _rms_scale(x, gain):
    ms = jnp.mean(x * x, axis=-1, keepdims=True)
    return x * lax.rsqrt(ms + NORM_EPS) * gain


def _sigmoid(x):
    return 0.5 * jnp.tanh(0.5 * x) + 0.5


def _cast_rider(w, n_i, n_j):
    r, c = w.shape
    blk = (r // n_i, c // n_j)
    assert r % n_i == 0 and c % n_j == 0 and blk[0] % 16 == 0 and blk[1] % LANES == 0

    def index(i, j):
        return i, jnp.minimum(j, n_j - 1)

    return pl.BlockSpec(blk, index), pl.BlockSpec(blk, index), jax.ShapeDtypeStruct(w.shape, BF16)


def _rider_bytes(w, n_i, n_j):
    return 2 * (w.size // (n_i * n_j)) * (4 + 2)


def _in_proj_kernel(*refs, gelu_tiles, n_riders):
    x_ref, g_ref, w_ref = refs[:3]
    cast_in = refs[3:3 + n_riders]
    o_ref, xb_ref = refs[3 + n_riders:5 + n_riders]
    cast_out = refs[5 + n_riders:5 + 2 * n_riders]
    h_sc, lin_sc = refs[5 + 2 * n_riders:]
    j = pl.program_id(1)

    @pl.when(j == 0)
    def _():
        h_sc[...] = _rms_scale(x_ref[...], g_ref[...]).astype(BF16)

    for src, dst in zip(cast_in, cast_out):
        dst[...] = src[...].astype(BF16)

    acc = jnp.dot(h_sc[...], w_ref[...], preferred_element_type=F32)

    @pl.when(j < gelu_tiles)
    def _():
        o_ref[...] = jax.nn.gelu(acc).astype(BF16)

    @pl.when(j == gelu_tiles)
    def _():
        o_ref[...] = acc.astype(BF16)
        lc, n_c = xb_ref.shape[0], xb_ref.shape[1]
        for k in range(lin_sc.shape[0]):
            cs = slice(k * LANES, (k + 1) * LANES)
            lin_sc[k] = acc[:, cs]
            for tau in range(lc):
                xb_ref[tau, :, cs] = lin_sc[k, pl.ds(tau, n_c, stride=lc), :]

    @pl.when(j > gelu_tiles)
    def _():
        o_ref[...] = _sigmoid(acc).astype(BF16)


def _in_proj(x2, gain, w_bf16, riders, *, gelu_width, lin_width, tm=512, tn=1024, rider_cols=8):
    t, d = x2.shape
    n = w_bf16.shape[1]
    n_tiles = n // tn
    assert gelu_width % tn == 0 and lin_width == tn and t % tm == 0 and n % tn == 0 and n_tiles >= rider_cols
    assert tm % (8 * S5_CHUNK) == 0
    n_i = t // tm
    rider_specs = [_cast_rider(w, n_i, rider_cols) for w in riders]
    vmem = (2 * tm * d * 4 + tm * d * 2 + 2 * d * tn * 2 + 2 * tm * tn * 2 + 3 * tm * tn * 4
            + 8 * tm * tn * 4 + sum(_rider_bytes(w, n_i, rider_cols) for w in riders))
    outs = pl.pallas_call(
        functools.partial(_in_proj_kernel, gelu_tiles=gelu_width // tn, n_riders=len(riders)),
        out_shape=(jax.ShapeDtypeStruct((t, n), BF16),
                   jax.ShapeDtypeStruct((S5_CHUNK, t // S5_CHUNK, lin_width), F32),
                   *[s[2] for s in rider_specs]),
        grid=(n_i, n_tiles),
        in_specs=[
            pl.BlockSpec((tm, d), lambda i, j: (i, 0)),
            pl.BlockSpec((1, d), lambda i, j: (0, 0)),
            pl.BlockSpec((d, tn), lambda i, j: (0, j)),
            *[s[0] for s in rider_specs],
        ],
        out_specs=(pl.BlockSpec((tm, tn), lambda i, j: (i, j)),
                   pl.BlockSpec((S5_CHUNK, tm // S5_CHUNK, lin_width), lambda i, j: (0, i, 0)),
                   *[s[1] for s in rider_specs]),
        scratch_shapes=[pltpu.VMEM((tm, d), BF16), pltpu.VMEM((lin_width // LANES, tm, LANES), F32)],
        compiler_params=pltpu.CompilerParams(
            dimension_semantics=("arbitrary", "arbitrary"), vmem_limit_bytes=vmem),
        name="in_proj",
    )(x2, gain, w_bf16, *riders)
    return outs[0], outs[1], outs[2:]


def _cmul(a, b):
    return a[0] * b[0] - a[1] * b[1], a[0] * b[1] + a[1] * b[0]


def _s5_tables(lam_re, lam_im, log_dt, b_re, b_im, c_re, c_im, d_skip):
    g, p = lam_re.shape
    hg = b_re.shape[-1]
    lc = S5_CHUNK
    npair = g // S5_PAIR
    dt = jnp.exp(log_dt)[:, None]
    mag = jnp.exp(lam_re * dt)
    a_bar = (mag * jnp.cos(lam_im * dt), mag * jnp.sin(lam_im * dt))
    den = lam_re * lam_re + lam_im * lam_im
    gain = (((a_bar[0] - 1.0) * lam_re + a_bar[1] * lam_im) / den,
            (a_bar[1] * lam_re - (a_bar[0] - 1.0) * lam_im) / den)
    b_bar = _cmul((gain[0][..., None], gain[1][..., None]), (b_re, b_im))
    pows = [(jnp.ones_like(mag), jnp.zeros_like(mag))]
    for _ in range(lc):
        pows.append(_cmul(pows[-1], a_bar))
    apow = (jnp.stack([q[0] for q in pows]), jnp.stack([q[1] for q in pows]))
    lanes = S5_PAIR * p
    own = (jnp.arange(S5_PAIR)[:, None] == (jnp.arange(lanes) // p)[None, :]).astype(F32)

    def pair_lanes(v):
        return jnp.transpose(v.reshape(v.shape[0], npair, lanes), (1, 0, 2))

    ap = (pair_lanes(apow[0]), pair_lanes(apow[1]))
    n_blk = S5_PAIR * hg

    def block_rows(v):
        return jnp.tile(v, (1, lc, 1))

    def power_rows(v):
        return jnp.repeat(v, n_blk, axis=1)

    bbt = [(jnp.transpose(v.reshape(npair, S5_PAIR, p, hg), (0, 3, 1, 2)).reshape(npair, 1, hg, lanes)
            * own[None, :, None, :]).reshape(npair, n_blk, lanes) for v in b_bar]
    e = _cmul((power_rows(ap[0][:, lc - 1::-1]), power_rows(ap[1][:, lc - 1::-1])),
              (block_rows(bbt[0]), block_rows(bbt[1])))
    w1_e = jnp.concatenate(e, axis=-1)
    cm = [jnp.transpose(v.reshape(npair, S5_PAIR, hg, p), (0, 1, 3, 2))[:, :, :, None, :]
          * jnp.eye(S5_PAIR, dtype=F32)[None, :, None, :, None] for v in (c_re, -c_im)]
    cm = jnp.concatenate([v.reshape(npair, lanes, S5_PAIR * hg) for v in cm], axis=1)
    kr = jnp.einsum('pxl,plo->pxo', w1_e, cm, precision=HIGHEST)
    skip = d_skip.reshape(npair, n_blk)[:, :, None] * jnp.eye(n_blk, dtype=F32)[None]
    kr = jnp.concatenate([kr[:, :(lc - 1) * n_blk], kr[:, (lc - 1) * n_blk:] + skip], axis=1).astype(BF16)
    kr = jnp.pad(kr, ((0, 0), (0, (lc - 1) * n_blk), (0, 0)))
    w1_y = jnp.concatenate(
        [kr[:, (lc - 1 - t) * n_blk:(2 * lc - 1 - t) * n_blk] for t in range(lc)], axis=-1)
    w1_e = w1_e.astype(BF16)
    ct = [(jnp.tile(v.reshape(npair, S5_PAIR, hg, p), (1, 1, 1, S5_PAIR)) * own[None, :, None, :]
           ).reshape(npair, n_blk, lanes) for v in (c_re, c_im)]
    cin = _cmul((block_rows(ct[0]), block_rows(ct[1])),
                (power_rows(ap[0][:, 1:lc + 1]), power_rows(ap[1][:, 1:lc + 1])))
    cin = jnp.concatenate([cin[0], -cin[1]], axis=-1)
    cin = jnp.swapaxes(cin.astype(BF16), 1, 2)
    a_lc = (apow[0][lc], apow[1][lc])
    cpows = [a_lc]
    for _ in range(S5_ROWS - 1):
        cpows.append(_cmul(cpows[-1], a_lc))
    a_tab = jnp.stack([jnp.stack([q[0] for q in cpows]), jnp.stack([q[1] for q in cpows])])
    a_tab = jnp.transpose(a_tab.reshape(2, S5_ROWS, npair, S5_PAIR * p), (2, 0, 1, 3))
    return w1_y, w1_e, cin, a_tab


def _s5_kernel(x_ref, w1y_ref, w1e_ref, cin_ref, a_ref, z_ref, y_sc, e_sc, sp_sc, *, n_chunks, n_pair,
               pair_ch):
    lc = S5_CHUNK

    for pp in range(n_pair):
        lanes = slice(pp * pair_ch, (pp + 1) * pair_ch)
        u = jnp.concatenate([x_ref[tau][:, lanes] for tau in range(lc)], axis=1).astype(BF16)
        y_sc[pp] = jnp.dot(u, w1y_ref[pp], preferred_element_type=F32)
        e_sc[pp] = jnp.dot(u, w1e_ref[pp], preferred_element_type=F32)

    rid = lax.broadcasted_iota(jnp.int32, (S5_ROWS, LANES), 0)

    def shift_rows(v, k, fill):
        return jnp.where(rid >= k, pltpu.roll(v, k, 0), fill)

    def body(i, carry):
        new = []
        r0 = pl.multiple_of(i * S5_ROWS, S5_ROWS)
        for pp in range(n_pair):
            c_re, c_im = carry[2 * pp], carry[2 * pp + 1]
            p_re, p_im = a_ref[pp, 0], a_ref[pp, 1]
            x_re = e_sc[pp, pl.ds(r0, S5_ROWS), 0:LANES]
            x_im = e_sc[pp, pl.ds(r0, S5_ROWS), LANES:2 * LANES]
            k = 1
            while k < S5_ROWS:
                k_re, k_im = p_re[k - 1:k, :], p_im[k - 1:k, :]
                sh_re, sh_im = shift_rows(x_re, k, 0.0), shift_rows(x_im, k, 0.0)
                x_re, x_im = x_re + k_re * sh_re - k_im * sh_im, x_im + k_re * sh_im + k_im * sh_re
                k *= 2
            s_re = x_re + p_re * c_re - p_im * c_im
            s_im = x_im + p_re * c_im + p_im * c_re
            sp_sc[pp, pl.ds(r0, S5_ROWS), 0:LANES] = shift_rows(s_re, 1, c_re)
            sp_sc[pp, pl.ds(r0, S5_ROWS), LANES:2 * LANES] = shift_rows(s_im, 1, c_im)
            new.append(s_re[S5_ROWS - 1:S5_ROWS, :])
            new.append(s_im[S5_ROWS - 1:S5_ROWS, :])
        return tuple(new)

    init = tuple(jnp.zeros((1, LANES), F32) for _ in range(2 * n_pair))
    lax.fori_loop(0, n_chunks // S5_ROWS, body, init)

    ys = [jax.nn.gelu(y_sc[pp] + jnp.dot(sp_sc[pp].astype(BF16), cin_ref[pp], preferred_element_type=F32))
          for pp in range(n_pair)]
    for t in range(lc):
        z_t = jnp.concatenate([y[:, t * pair_ch:(t + 1) * pair_ch] for y in ys], axis=1)
        z_ref[pl.ds(t, n_chunks, stride=lc), :] = z_t


def _s5_branch(x_b, tables, *, n_batch, seq):
    w1_y, w1_e, cin, a_tab = tables
    npair, kw, _ = w1_y.shape
    lc, n_rows, bw = x_b.shape
    t = lc * n_rows
    n_chunks = seq // lc
    pair_ch = kw // lc
    n_pair = LANES // pair_ch
    assert lc == S5_CHUNK and n_chunks % S5_ROWS == 0 and bw % LANES == 0 and n_rows == n_batch * n_chunks
    return pl.pallas_call(
        functools.partial(_s5_kernel, n_chunks=n_chunks, n_pair=n_pair, pair_ch=pair_ch),
        out_shape=jax.ShapeDtypeStruct((t, bw), F32),
        grid=(bw // LANES, n_batch),
        in_specs=[
            pl.BlockSpec((lc, n_chunks, LANES), lambda c, b: (0, b, c)),
            pl.BlockSpec((n_pair, kw, kw), lambda c, b: (c, 0, 0)),
            pl.BlockSpec((n_pair, kw, 2 * LANES), lambda c, b: (c, 0, 0)),
            pl.BlockSpec((n_pair, 2 * LANES, kw), lambda c, b: (c, 0, 0)),
            pl.BlockSpec((n_pair, 2, S5_ROWS, LANES), lambda c, b: (c, 0, 0, 0)),
        ],
        out_specs=pl.BlockSpec((seq, LANES), lambda c, b: (b, c)),
        scratch_shapes=[pltpu.VMEM((n_pair, n_chunks, kw), F32),
                        pltpu.VMEM((n_pair, n_chunks, 2 * LANES), F32),
                        pltpu.VMEM((n_pair, n_chunks, 2 * LANES), F32)],
        compiler_params=pltpu.CompilerParams(dimension_semantics=("arbitrary", "arbitrary")),
        name="s5",
    )(x_b, w1_y, w1_e, cin, a_tab)


def _branches_kernel(u_ref, v_ref, ga_ref, gb_ref, zb_ref, lng_ref, lnb_ref, wsp_ref, bsp_ref,
                     wpa_ref, wga_ref, wgb_ref, cast_in_ref, o_ref, cast_out_ref, s_sc, *, n_heads, head_dim):
    j = pl.program_id(1)
    cast_out_ref[...] = cast_in_ref[...].astype(BF16)

    def glu_branch():
        zb = zb_ref[...].astype(BF16)
        glu_a = jnp.dot(zb, wga_ref[...], preferred_element_type=F32)
        glu_b = jnp.dot(zb, wgb_ref[...], preferred_element_type=F32)
        return gb_ref[...].astype(F32) * (glu_a * _sigmoid(glu_b))

    def merge(mix_b):
        br_a = jnp.dot(s_sc[...], wpa_ref[...], preferred_element_type=F32)
        o_ref[...] = (ga_ref[...].astype(F32) * br_a + mix_b).astype(BF16)

    def spatial_gating():
        v = v_ref[...].astype(F32)
        mu = jnp.mean(v, axis=-1, keepdims=True)
        vc = v - mu
        var = jnp.mean(vc * vc, axis=-1, keepdims=True)
        vn = (vc * lax.rsqrt(var + NORM_EPS) * lng_ref[...] + lnb_ref[...]).astype(BF16)
        n_c = v.shape[0] // A_CHUNK
        row = lax.broadcasted_iota(jnp.int32, (A_CHUNK, A_CHUNK), 0)
        col = lax.broadcasted_iota(jnp.int32, (A_CHUNK, A_CHUNK), 1)
        causal = row >= col
        for h in range(n_heads):
            hs = slice(h * head_dim, (h + 1) * head_dim)
            w = jnp.where(causal, wsp_ref[h], jnp.zeros((), BF16))
            rhs = jnp.concatenate(
                [vn[c * A_CHUNK:(c + 1) * A_CHUNK, hs] for c in range(n_c)], axis=1)
            mixed = jnp.dot(w, rhs, preferred_element_type=F32)
            bias = bsp_ref[:, hs]
            for c in range(n_c):
                rs = slice(c * A_CHUNK, (c + 1) * A_CHUNK)
                u_blk = u_ref[rs, hs].astype(F32)
                s_sc[rs, hs] = (u_blk * (mixed[:, c * head_dim:(c + 1) * head_dim] + bias)).astype(BF16)

    @pl.when(j == 0)
    def _():
        mix_b = glu_branch()
        spatial_gating()
        merge(mix_b)

    @pl.when(j > 0)
    def _():
        merge(glu_branch())


def _branches(proj, z_b, ln_g, ln_b, w_sp, b_sp_full, wpa, wga, wgb, rider, *, a_width, b_width, d_model,
              tm=512, tn=1024):
    t = proj.shape[0]
    n_heads = w_sp.shape[0]
    head_dim = a_width // n_heads
    ga_off = (2 * a_width + b_width) // tn
    gb_off = (2 * a_width + b_width + d_model) // tn
    assert (2 * a_width + b_width) % tn == 0 and d_model % tn == 0 and t % tm == 0 and tm % A_CHUNK == 0
    n_i, n_j = t // tm, d_model // tn
    rider_in, rider_out, rider_shape = _cast_rider(rider, n_i, n_j)
    vmem = (2 * 2 * tm * a_width * 2 + 2 * 2 * tm * tn * 2 + 2 * tm * b_width * 4
            + 2 * a_width * tn * 2 + 2 * 2 * b_width * tn * 2 + tm * a_width * 2 + 2 * tm * tn * 2
            + 4 * tm * a_width * 4 + 6 * tm * tn * 4 + _rider_bytes(rider, n_i, n_j))
    return pl.pallas_call(
        functools.partial(_branches_kernel, n_heads=n_heads, head_dim=head_dim),
        out_shape=(jax.ShapeDtypeStruct((t, d_model), BF16), rider_shape),
        grid=(n_i, n_j),
        in_specs=[
            pl.BlockSpec((tm, a_width), lambda i, j: (i, 0)),
            pl.BlockSpec((tm, a_width), lambda i, j: (i, 1)),
            pl.BlockSpec((tm, tn), lambda i, j: (i, ga_off + j)),
            pl.BlockSpec((tm, tn), lambda i, j: (i, gb_off + j)),
            pl.BlockSpec((tm, b_width), lambda i, j: (i, 0)),
            pl.BlockSpec((1, a_width), lambda i, j: (0, 0)),
            pl.BlockSpec((1, a_width), lambda i, j: (0, 0)),
            pl.BlockSpec((n_heads, A_CHUNK, A_CHUNK), lambda i, j: (0, 0, 0)),
            pl.BlockSpec((A_CHUNK, a_width), lambda i, j: (0, 0)),
            pl.BlockSpec((a_width, tn), lambda i, j: (0, j)),
            pl.BlockSpec((b_width, tn), lambda i, j: (0, j)),
            pl.BlockSpec((b_width, tn), lambda i, j: (0, j)),
            rider_in,
        ],
        out_specs=(pl.BlockSpec((tm, tn), lambda i, j: (i, j)), rider_out),
        scratch_shapes=[pltpu.VMEM((tm, a_width), BF16)],
        compiler_params=pltpu.CompilerParams(
            dimension_semantics=("arbitrary", "arbitrary"), vmem_limit_bytes=vmem),
        name="branches",
    )(proj, proj, proj, proj, z_b, ln_g, ln_b, w_sp, b_sp_full, wpa, wga, wgb, rider)


def _out_proj_kernel(m_ref, w_ref, x_ref, g_ref, o_ref, x_sc, *, n_tiles, tn):
    j = pl.program_id(1)
    for k in range(n_tiles):
        @pl.when(j == k)
        def _(k=k):
            cs = slice(k * tn, (k + 1) * tn)
            o_ref[:, cs] = jnp.dot(m_ref[...], w_ref[...], preferred_element_type=F32)
            x_sc[:, cs] = x_ref[...]

    @pl.when(j == n_tiles - 1)
    def _():
        ssq = jnp.zeros((o_ref.shape[0], 1), F32)
        for k in range(n_tiles):
            a = o_ref[:, k * tn:(k + 1) * tn]
            ssq = ssq + jnp.sum(a * a, axis=-1, keepdims=True)
        inv = lax.rsqrt(ssq / (n_tiles * tn) + NORM_EPS)
        for k in range(n_tiles):
            cs = slice(k * tn, (k + 1) * tn)
            o_ref[:, cs] = x_sc[:, cs] + o_ref[:, cs] * inv * g_ref[:, cs]


def _out_proj(mix_in, w_bf16, x2, gain, *, tm=512, tn=1024):
    t, d = x2.shape
    n_tiles = d // tn
    vmem = (2 * tm * d * 2 + 2 * d * tn * 2 + 2 * tm * tn * 4 + 2 * tm * d * 4 + tm * d * 4 + 4 * tm * tn * 4)
    return pl.pallas_call(
        functools.partial(_out_proj_kernel, n_tiles=n_tiles, tn=tn),
        out_shape=jax.ShapeDtypeStruct((t, d), F32),
        grid=(t // tm, n_tiles),
        in_specs=[
            pl.BlockSpec((tm, d), lambda i, j: (i, 0)),
            pl.BlockSpec((d, tn), lambda i, j: (0, j)),
            pl.BlockSpec((tm, tn), lambda i, j: (i, j)),
            pl.BlockSpec((1, d), lambda i, j: (0, 0)),
        ],
        out_specs=pl.BlockSpec((tm, d), lambda i, j: (i, 0)),
        scratch_shapes=[pltpu.VMEM((tm, d), F32)],
        compiler_params=pltpu.CompilerParams(
            dimension_semantics=("arbitrary", "arbitrary"), vmem_limit_bytes=vmem),
        name="out_proj",
    )(mix_in, w_bf16, x2, gain)


def _mlp_kernel(x_ref, gpre_ref, wu_ref, wd_ref, gpost_ref, o_ref, h_sc, a_sc, *, n_f, tn):
    s = pl.program_id(1)
    cur = s % 2

    def up(slot):
        a = jnp.dot(h_sc[...], wu_ref[...], preferred_element_type=F32)
        a_sc[slot] = jnp.square(jnp.maximum(a, 0.0)).astype(BF16)

    def down(slot):
        a = a_sc[slot]
        for k in range(o_ref.shape[1] // tn):
            cs = slice(k * tn, (k + 1) * tn)
            o_ref[:, cs] += jnp.dot(a, wd_ref[:, cs], preferred_element_type=F32)

    @pl.when(s == 0)
    def _():
        h_sc[...] = _rms_scale(x_ref[...], gpre_ref[...]).astype(BF16)
        o_ref[...] = jnp.zeros(o_ref.shape, F32)
        up(cur)

    @pl.when(jnp.logical_and(s > 0, s < n_f))
    def _():
        down(1 - cur)
        up(cur)

    @pl.when(s == n_f)
    def _():
        down(1 - cur)
        o_ref[...] = x_ref[...] + _rms_scale(o_ref[...], gpost_ref[...])


def _mlp(x1, g_pre, wu, wd, g_post, *, tm=512, tf=512, tn=1024):
    t, d = x1.shape
    d_ff = wu.shape[1]
    n_f = d_ff // tf
    vmem = (2 * tm * d * 4 + 2 * tm * d * 4 + tm * d * 2 + 2 * 2 * d * tf * 2 + 2 * tm * tf * 2
            + 3 * tm * tf * 4 + 2 * tm * tn * 4 + tm * d * 4)
    return pl.pallas_call(
        functools.partial(_mlp_kernel, n_f=n_f, tn=tn),
        out_shape=jax.ShapeDtypeStruct((t, d), F32),
        grid=(t // tm, n_f + 1),
        in_specs=[
            pl.BlockSpec((tm, d), lambda i, s: (i, 0)),
            pl.BlockSpec((1, d), lambda i, s: (0, 0)),
            pl.BlockSpec((d, tf), lambda i, s: (0, jnp.minimum(s, n_f - 1))),
            pl.BlockSpec((tf, d), lambda i, s: (jnp.maximum(s - 1, 0), 0)),
            pl.BlockSpec((1, d), lambda i, s: (0, 0)),
        ],
        out_specs=pl.BlockSpec((tm, d), lambda i, s: (i, 0)),
        scratch_shapes=[pltpu.VMEM((tm, d), BF16), pltpu.VMEM((2, tm, tf), BF16)],
        compiler_params=pltpu.CompilerParams(
            dimension_semantics=("arbitrary", "arbitrary"), vmem_limit_bytes=vmem),
        name="mlp",
    )(x1, g_pre, wu, wd, g_post)


def kernel(x, norm_mix_pre, w_in, v_norm_g, v_norm_b, w_spatial, b_spatial, w_proj_a, lam_re, lam_im,
           log_dt, b_re, b_im, c_re, c_im, d_skip, w_glu_a, w_glu_b, w_out, norm_mix_post, norm_mlp_pre,
           w_ff_up, w_ff_down, norm_mlp_post):
    n_batch, seq, d_model = x.shape
    depth = w_in.shape[0]
    a_width = w_proj_a.shape[1]
    b_width = w_glu_a.shape[1]
    n_heads = w_spatial.shape[1]
    head_dim = a_width // n_heads
    x2 = x.reshape(n_batch * seq, d_model)
    for l in range(depth):
        proj, x_b, (wu, wo, wpa, wga, wgb) = _in_proj(
            x2, norm_mix_pre[l][None], w_in[l].astype(BF16),
            (w_ff_up[l], w_out[l], w_proj_a[l], w_glu_a[l], w_glu_b[l]),
            gelu_width=2 * a_width, lin_width=b_width)
        tables = _s5_tables(lam_re[l], lam_im[l], log_dt[l], b_re[l], b_im[l], c_re[l], c_im[l], d_skip[l])
        z_b = _s5_branch(x_b, tables, n_batch=n_batch, seq=seq)
        b_sp_full = jnp.repeat(b_spatial[l].T, head_dim, axis=1)
        mix_in, wd = _branches(proj, z_b, v_norm_g[l][None], v_norm_b[l][None], w_spatial[l].astype(BF16),
                               b_sp_full, wpa, wga, wgb, w_ff_down[l],
                               a_width=a_width, b_width=b_width, d_model=d_model)
        x1 = _out_proj(mix_in, wo, x2, norm_mix_post[l][None])
        x2 = _mlp(x1, norm_mlp_pre[l][None], wu, wd, norm_mlp_post[l][None])
    return x2.reshape(n_batch, seq, d_model)
```

```python
import functools

import jax
import jax.numpy as jnp
from jax import lax
from jax.experimental import pallas as pl
from jax.experimental.pallas import tpu as pltpu

F32 = jnp.float32
BF16 = jnp.bfloat16
NORM_EPS = 1e-6
HIGHEST = lax.Precision.HIGHEST

LANES = 128
A_CHUNK = 128
S5_CHUNK = 16
S5_PAIR = 2
S5_ROWS = 8


def _rms_scale(x, gain):
    ms = jnp.mean(x * x, axis=-1, keepdims=True)
    return x * lax.rsqrt(ms + NORM_EPS) * gain


def _sigmoid(x):
    return 0.5 * jnp.tanh(0.5 * x) + 0.5


def _cast_rider(w, n_i, n_j):
    r, c = w.shape
    blk = (r // n_i, c // n_j)
    assert r % n_i == 0 and c % n_j == 0 and blk[0] % 16 == 0 and blk[1] % LANES == 0

    def index(i, j):
        return i, jnp.minimum(j, n_j - 1)

    return pl.BlockSpec(blk, index), pl.BlockSpec(blk, index), jax.ShapeDtypeStruct(w.shape, BF16)


def _rider_bytes(w, n_i, n_j):
    return 2 * (w.size // (n_i * n_j)) * (4 + 2)


def _in_proj_kernel(*refs, gelu_tiles, n_riders):
    x_ref, g_ref, w_ref = refs[:3]
    cast_in = refs[3:3 + n_riders]
    o_ref, xb_ref = refs[3 + n_riders:5 + n_riders]
    cast_out = refs[5 + n_riders:5 + 2 * n_riders]
    h_sc, lin_sc = refs[5 + 2 * n_riders:]
    j = pl.program_id(1)

    @pl.when(j == 0)
    def _():
        h_sc[...] = _rms_scale(x_ref[...], g_ref[...]).astype(BF16)

    for src, dst in zip(cast_in, cast_out):
        dst[...] = src[...].astype(BF16)

    acc = jnp.dot(h_sc[...], w_ref[...], preferred_element_type=F32)

    @pl.when(j < gelu_tiles)
    def _():
        o_ref[...] = jax.nn.gelu(acc).astype(BF16)

    @pl.when(j == gelu_tiles)
    def _():
        o_ref[...] = acc.astype(BF16)
        lc, n_c = xb_ref.shape[0], xb_ref.shape[1]
        for k in range(lin_sc.shape[0]):
            cs = slice(k * LANES, (k + 1) * LANES)
            lin_sc[k] = acc[:, cs]
            for tau in range(lc):
                xb_ref[tau, :, cs] = lin_sc[k, pl.ds(tau, n_c, stride=lc), :]

    @pl.when(j > gelu_tiles)
    def _():
        o_ref[...] = _sigmoid(acc).astype(BF16)


def _in_proj(x2, gain, w_bf16, riders, *, gelu_width, lin_width, tm=512, tn=1024, rider_cols=8):
    t, d = x2.shape
    n = w_bf16.shape[1]
    n_tiles = n // tn
    assert gelu_width % tn == 0 and lin_width == tn and t % tm == 0 and n % tn == 0 and n_tiles >= rider_cols
    assert tm % (8 * S5_CHUNK) == 0
    n_i = t // tm
    rider_specs = [_cast_rider(w, n_i, rider_cols) for w in riders]
    vmem = (2 * tm * d * 4 + tm * d * 2 + 2 * d * tn * 2 + 2 * tm * tn * 2 + 3 * tm * tn * 4
            + 8 * tm * tn * 4 + sum(_rider_bytes(w, n_i, rider_cols) for w in riders))
    outs = pl.pallas_call(
        functools.partial(_in_proj_kernel, gelu_tiles=gelu_width // tn, n_riders=len(riders)),
        out_shape=(jax.ShapeDtypeStruct((t, n), BF16),
                   jax.ShapeDtypeStruct((S5_CHUNK, t // S5_CHUNK, lin_width), F32),
                   *[s[2] for s in rider_specs]),
        grid=(n_i, n_tiles),
        in_specs=[
            pl.BlockSpec((tm, d), lambda i, j: (i, 0)),
            pl.BlockSpec((1, d), lambda i, j: (0, 0)),
            pl.BlockSpec((d, tn), lambda i, j: (0, j)),
            *[s[0] for s in rider_specs],
        ],
        out_specs=(pl.BlockSpec((tm, tn), lambda i, j: (i, j)),
                   pl.BlockSpec((S5_CHUNK, tm // S5_CHUNK, lin_width), lambda i, j: (0, i, 0)),
                   *[s[1] for s in rider_specs]),
        scratch_shapes=[pltpu.VMEM((tm, d), BF16), pltpu.VMEM((lin_width // LANES, tm, LANES), F32)],
        compiler_params=pltpu.CompilerParams(
            dimension_semantics=("arbitrary", "arbitrary"), vmem_limit_bytes=vmem),
        name="in_proj",
    )(x2, gain, w_bf16, *riders)
    return outs[0], outs[1], outs[2:]


def _cmul(a, b):
    return a[0] * b[0] - a[1] * b[1], a[0] * b[1] + a[1] * b[0]


def _s5_tables(lam_re, lam_im, log_dt, b_re, b_im, c_re, c_im, d_skip):
    g, p = lam_re.shape
    hg = b_re.shape[-1]
    lc = S5_CHUNK
    npair = g // S5_PAIR
    dt = jnp.exp(log_dt)[:, None]
    mag = jnp.exp(lam_re * dt)
    a_bar = (mag * jnp.cos(lam_im * dt), mag * jnp.sin(lam_im * dt))
    den = lam_re * lam_re + lam_im * lam_im
    gain = (((a_bar[0] - 1.0) * lam_re + a_bar[1] * lam_im) / den,
            (a_bar[1] * lam_re - (a_bar[0] - 1.0) * lam_im) / den)
    b_bar = _cmul((gain[0][..., None], gain[1][..., None]), (b_re, b_im))
    pows = [(jnp.ones_like(mag), jnp.zeros_like(mag))]
    for _ in range(lc):
        pows.append(_cmul(pows[-1], a_bar))
    apow = (jnp.stack([q[0] for q in pows]), jnp.stack([q[1] for q in pows]))
    lanes = S5_PAIR * p
    own = (jnp.arange(S5_PAIR)[:, None] == (jnp.arange(lanes) // p)[None, :]).astype(F32)

    def pair_lanes(v):
        return jnp.transpose(v.reshape(v.shape[0], npair, lanes), (1, 0, 2))

    ap = (pair_lanes(apow[0]), pair_lanes(apow[1]))
    n_blk = S5_PAIR * hg

    def block_rows(v):
        return jnp.tile(v, (1, lc, 1))

    def power_rows(v):
        return jnp.repeat(v, n_blk, axis=1)

    bbt = [(jnp.transpose(v.reshape(npair, S5_PAIR, p, hg), (0, 3, 1, 2)).reshape(npair, 1, hg, lanes)
            * own[None, :, None, :]).reshape(npair, n_blk, lanes) for v in b_bar]
    e = _cmul((power_rows(ap[0][:, lc - 1::-1]), power_rows(ap[1][:, lc - 1::-1])),
              (block_rows(bbt[0]), block_rows(bbt[1])))
    w1_e = jnp.concatenate(e, axis=-1)
    cm = [jnp.transpose(v.reshape(npair, S5_PAIR, hg, p), (0, 1, 3, 2))[:, :, :, None, :]
          * jnp.eye(S5_PAIR, dtype=F32)[None, :, None, :, None] for v in (c_re, -c_im)]
    cm = jnp.concatenate([v.reshape(npair, lanes, S5_PAIR * hg) for v in cm], axis=1)
    kr = jnp.einsum('pxl,plo->pxo', w1_e, cm, precision=HIGHEST)
    skip = d_skip.reshape(npair, n_blk)[:, :, None] * jnp.eye(n_blk, dtype=F32)[None]
    kr = jnp.concatenate([kr[:, :(lc - 1) * n_blk], kr[:, (lc - 1) * n_blk:] + skip], axis=1)
    kr = jnp.pad(kr, ((0, 0), (0, (lc - 1) * n_blk), (0, 0)))
    w1_e = w1_e.astype(BF16)
    ct = [(jnp.tile(v.reshape(npair, S5_PAIR, hg, p), (1, 1, 1, S5_PAIR)) * own[None, :, None, :]
           ).reshape(npair, n_blk, lanes) for v in (c_re, c_im)]
    cin = _cmul((block_rows(ct[0]), block_rows(ct[1])),
                (power_rows(ap[0][:, 1:lc + 1]), power_rows(ap[1][:, 1:lc + 1])))
    cin = jnp.concatenate([cin[0], -cin[1]], axis=-1)
    cin = jnp.swapaxes(cin.astype(BF16), 1, 2)
    a_lc = (apow[0][lc], apow[1][lc])
    cpows = [a_lc]
    for _ in range(S5_ROWS - 1):
        cpows.append(_cmul(cpows[-1], a_lc))
    a_tab = jnp.stack([jnp.stack([q[0] for q in cpows]), jnp.stack([q[1] for q in cpows])])
    a_tab = jnp.transpose(a_tab.reshape(2, S5_ROWS, npair, S5_PAIR * p), (2, 0, 1, 3))
    return kr, w1_e, cin, a_tab


def _s5_kernel(x_ref, kr_ref, w1e_ref, cin_ref, a_ref, z_ref, w1y_sc, y_sc, e_sc, sp_sc, *, n_chunks, n_pair,
               pair_ch):
    lc = S5_CHUNK

    @pl.when(pl.program_id(1) == 0)
    def _():
        rows = lc * pair_ch
        for pp in range(n_pair):
            w1y_sc[pp] = jnp.concatenate(
                [kr_ref[pp, (lc - 1 - t) * pair_ch:(lc - 1 - t) * pair_ch + rows, :] for t in range(lc)],
                axis=1).astype(BF16)

    for pp in range(n_pair):
        lanes = slice(pp * pair_ch, (pp + 1) * pair_ch)
        u = jnp.concatenate([x_ref[tau][:, lanes] for tau in range(lc)], axis=1).astype(BF16)
        y_sc[pp] = jnp.dot(u, w1y_sc[pp], preferred_element_type=F32)
        e_sc[pp] = jnp.dot(u, w1e_ref[pp], preferred_element_type=F32)

    rid = lax.broadcasted_iota(jnp.int32, (S5_ROWS, LANES), 0)

    def shift_rows(v, k, fill):
        return jnp.where(rid >= k, pltpu.roll(v, k, 0), fill)

    def body(i, carry):
        new = []
        r0 = pl.multiple_of(i * S5_ROWS, S5_ROWS)
        for pp in range(n_pair):
            c_re, c_im = carry[2 * pp], carry[2 * pp + 1]
            p_re, p_im = a_ref[pp, 0], a_ref[pp, 1]
            x_re = e_sc[pp, pl.ds(r0, S5_ROWS), 0:LANES]
            x_im = e_sc[pp, pl.ds(r0, S5_ROWS), LANES:2 * LANES]
            k = 1
            while k < S5_ROWS:
                k_re, k_im = p_re[k - 1:k, :], p_im[k - 1:k, :]
                sh_re, sh_im = shift_rows(x_re, k, 0.0), shift_rows(x_im, k, 0.0)
                x_re, x_im = x_re + k_re * sh_re - k_im * sh_im, x_im + k_re * sh_im + k_im * sh_re
                k *= 2
            s_re = x_re + p_re * c_re - p_im * c_im
            s_im = x_im + p_re * c_im + p_im * c_re
            sp_sc[pp, pl.ds(r0, S5_ROWS), 0:LANES] = shift_rows(s_re, 1, c_re)
            sp_sc[pp, pl.ds(r0, S5_ROWS), LANES:2 * LANES] = shift_rows(s_im, 1, c_im)
            new.append(s_re[S5_ROWS - 1:S5_ROWS, :])
            new.append(s_im[S5_ROWS - 1:S5_ROWS, :])
        return tuple(new)

    init = tuple(jnp.zeros((1, LANES), F32) for _ in range(2 * n_pair))
    lax.fori_loop(0, n_chunks // S5_ROWS, body, init)

    ys = [jax.nn.gelu(y_sc[pp] + jnp.dot(sp_sc[pp].astype(BF16), cin_ref[pp], preferred_element_type=F32))
          for pp in range(n_pair)]
    for t in range(lc):
        z_t = jnp.concatenate([y[:, t * pair_ch:(t + 1) * pair_ch] for y in ys], axis=1)
        z_ref[pl.ds(t, n_chunks, stride=lc), :] = z_t


def _s5_branch(x_b, tables, *, n_batch, seq):
    kr, w1_e, cin, a_tab = tables
    npair, kw, _ = w1_e.shape
    lc, n_rows, bw = x_b.shape
    t = lc * n_rows
    n_chunks = seq // lc
    pair_ch = kw // lc
    n_pair = LANES // pair_ch
    assert lc == S5_CHUNK and n_chunks % S5_ROWS == 0 and bw % LANES == 0 and n_rows == n_batch * n_chunks
    assert kr.shape == (npair, (2 * lc - 1) * pair_ch, pair_ch)
    return pl.pallas_call(
        functools.partial(_s5_kernel, n_chunks=n_chunks, n_pair=n_pair, pair_ch=pair_ch),
        out_shape=jax.ShapeDtypeStruct((t, bw), F32),
        grid=(bw // LANES, n_batch),
        in_specs=[
            pl.BlockSpec((lc, n_chunks, LANES), lambda c, b: (0, b, c)),
            pl.BlockSpec((n_pair, (2 * lc - 1) * pair_ch, pair_ch), lambda c, b: (c, 0, 0)),
            pl.BlockSpec((n_pair, kw, 2 * LANES), lambda c, b: (c, 0, 0)),
            pl.BlockSpec((n_pair, 2 * LANES, kw), lambda c, b: (c, 0, 0)),
            pl.BlockSpec((n_pair, 2, S5_ROWS, LANES), lambda c, b: (c, 0, 0, 0)),
        ],
        out_specs=pl.BlockSpec((seq, LANES), lambda c, b: (b, c)),
        scratch_shapes=[pltpu.VMEM((n_pair, kw, kw), BF16),
                        pltpu.VMEM((n_pair, n_chunks, kw), F32),
                        pltpu.VMEM((n_pair, n_chunks, 2 * LANES), F32),
                        pltpu.VMEM((n_pair, n_chunks, 2 * LANES), F32)],
        compiler_params=pltpu.CompilerParams(dimension_semantics=("arbitrary", "arbitrary")),
        name="s5",
    )(x_b, kr, w1_e, cin, a_tab)


def _branches_kernel(u_ref, v_ref, ga_ref, gb_ref, zb_ref, lng_ref, lnb_ref, wsp_ref, bsp_ref,
                     wpa_ref, wga_ref, wgb_ref, cast_in_ref, o_ref, cast_out_ref, s_sc, *, n_heads, head_dim):
    j = pl.program_id(1)
    cast_out_ref[...] = cast_in_ref[...].astype(BF16)

    def glu_branch():
        zb = zb_ref[...].astype(BF16)
        glu_a = jnp.dot(zb, wga_ref[...], preferred_element_type=F32)
        glu_b = jnp.dot(zb, wgb_ref[...], preferred_element_type=F32)
        return gb_ref[...].astype(F32) * (glu_a * _sigmoid(glu_b))

    def merge(mix_b):
        br_a = jnp.dot(s_sc[...], wpa_ref[...], preferred_element_type=F32)
        o_ref[...] = (ga_ref[...].astype(F32) * br_a + mix_b).astype(BF16)

    def spatial_gating():
        v = v_ref[...].astype(F32)
        mu = jnp.mean(v, axis=-1, keepdims=True)
        vc = v - mu
        var = jnp.mean(vc * vc, axis=-1, keepdims=True)
        vn = (vc * lax.rsqrt(var + NORM_EPS) * lng_ref[...] + lnb_ref[...]).astype(BF16)
        n_c = v.shape[0] // A_CHUNK
        row = lax.broadcasted_iota(jnp.int32, (A_CHUNK, A_CHUNK), 0)
        col = lax.broadcasted_iota(jnp.int32, (A_CHUNK, A_CHUNK), 1)
        causal = row >= col
        for h in range(n_heads):
            hs = slice(h * head_dim, (h + 1) * head_dim)
            w = jnp.where(causal, wsp_ref[h], jnp.zeros((), BF16))
            rhs = jnp.concatenate(
                [vn[c * A_CHUNK:(c + 1) * A_CHUNK, hs] for c in range(n_c)], axis=1)
            mixed = jnp.dot(w, rhs, preferred_element_type=F32)
            bias = bsp_ref[:, hs]
            for c in range(n_c):
                rs = slice(c * A_CHUNK, (c + 1) * A_CHUNK)
                u_blk = u_ref[rs, hs].astype(F32)
                s_sc[rs, hs] = (u_blk * (mixed[:, c * head_dim:(c + 1) * head_dim] + bias)).astype(BF16)

    @pl.when(j == 0)
    def _():
        mix_b = glu_branch()
        spatial_gating()
        merge(mix_b)

    @pl.when(j > 0)
    def _():
        merge(glu_branch())


def _branches(proj, z_b, ln_g, ln_b, w_sp, b_sp_full, wpa, wga, wgb, rider, *, a_width, b_width, d_model,
              tm=512, tn=1024):
    t = proj.shape[0]
    n_heads = w_sp.shape[0]
    head_dim = a_width // n_heads
    ga_off = (2 * a_width + b_width) // tn
    gb_off = (2 * a_width + b_width + d_model) // tn
    assert (2 * a_width + b_width) % tn == 0 and d_model % tn == 0 and t % tm == 0 and tm % A_CHUNK == 0
    n_i, n_j = t // tm, d_model // tn
    rider_in, rider_out, rider_shape = _cast_rider(rider, n_i, n_j)
    vmem = (2 * 2 * tm * a_width * 2 + 2 * 2 * tm * tn * 2 + 2 * tm * b_width * 4
            + 2 * a_width * tn * 2 + 2 * 2 * b_width * tn * 2 + tm * a_width * 2 + 2 * tm * tn * 2
            + 4 * tm * a_width * 4 + 6 * tm * tn * 4 + _rider_bytes(rider, n_i, n_j))
    return pl.pallas_call(
        functools.partial(_branches_kernel, n_heads=n_heads, head_dim=head_dim),
        out_shape=(jax.ShapeDtypeStruct((t, d_model), BF16), rider_shape),
        grid=(n_i, n_j),
        in_specs=[
            pl.BlockSpec((tm, a_width), lambda i, j: (i, 0)),
            pl.BlockSpec((tm, a_width), lambda i, j: (i, 1)),
            pl.BlockSpec((tm, tn), lambda i, j: (i, ga_off + j)),
            pl.BlockSpec((tm, tn), lambda i, j: (i, gb_off + j)),
            pl.BlockSpec((tm, b_width), lambda i, j: (i, 0)),
            pl.BlockSpec((1, a_width), lambda i, j: (0, 0)),
            pl.BlockSpec((1, a_width), lambda i, j: (0, 0)),
            pl.BlockSpec((n_heads, A_CHUNK, A_CHUNK), lambda i, j: (0, 0, 0)),
            pl.BlockSpec((A_CHUNK, a_width), lambda i, j: (0, 0)),
            pl.BlockSpec((a_width, tn), lambda i, j: (0, j)),
            pl.BlockSpec((b_width, tn), lambda i, j: (0, j)),
            pl.BlockSpec((b_width, tn), lambda i, j: (0, j)),
            rider_in,
        ],
        out_specs=(pl.BlockSpec((tm, tn), lambda i, j: (i, j)), rider_out),
        scratch_shapes=[pltpu.VMEM((tm, a_width), BF16)],
        compiler_params=pltpu.CompilerParams(
            dimension_semantics=("arbitrary", "arbitrary"), vmem_limit_bytes=vmem),
        name="branches",
    )(proj, proj, proj, proj, z_b, ln_g, ln_b, w_sp, b_sp_full, wpa, wga, wgb, rider)


def _out_proj_kernel(m_ref, w_ref, x_ref, g_ref, o_ref, x_sc, *, n_tiles, tn):
    j = pl.program_id(1)
    for k in range(n_tiles):
        @pl.when(j == k)
        def _(k=k):
            cs = slice(k * tn, (k + 1) * tn)
            o_ref[:, cs] = jnp.dot(m_ref[...], w_ref[...], preferred_element_type=F32)
            x_sc[:, cs] = x_ref[...]

    @pl.when(j == n_tiles - 1)
    def _():
        ssq = jnp.zeros((o_ref.shape[0], 1), F32)
        for k in range(n_tiles):
            a = o_ref[:, k * tn:(k + 1) * tn]
            ssq = ssq + jnp.sum(a * a, axis=-1, keepdims=True)
        inv = lax.rsqrt(ssq / (n_tiles * tn) + NORM_EPS)
        for k in range(n_tiles):
            cs = slice(k * tn, (k + 1) * tn)
            o_ref[:, cs] = x_sc[:, cs] + o_ref[:, cs] * inv * g_ref[:, cs]


def _out_proj(mix_in, w_bf16, x2, gain, *, tm=512, tn=1024):
    t, d = x2.shape
    n_tiles = d // tn
    vmem = (2 * tm * d * 2 + 2 * d * tn * 2 + 2 * tm * tn * 4 + 2 * tm * d * 4 + tm * d * 4 + 4 * tm * tn * 4)
    return pl.pallas_call(
        functools.partial(_out_proj_kernel, n_tiles=n_tiles, tn=tn),
        out_shape=jax.ShapeDtypeStruct((t, d), F32),
        grid=(t // tm, n_tiles),
        in_specs=[
            pl.BlockSpec((tm, d), lambda i, j: (i, 0)),
            pl.BlockSpec((d, tn), lambda i, j: (0, j)),
            pl.BlockSpec((tm, tn), lambda i, j: (i, j)),
            pl.BlockSpec((1, d), lambda i, j: (0, 0)),
        ],
        out_specs=pl.BlockSpec((tm, d), lambda i, j: (i, 0)),
        scratch_shapes=[pltpu.VMEM((tm, d), F32)],
        compiler_params=pltpu.CompilerParams(
            dimension_semantics=("arbitrary", "arbitrary"), vmem_limit_bytes=vmem),
        name="out_proj",
    )(mix_in, w_bf16, x2, gain)


def _mlp_kernel(x_ref, gpre_ref, wu_ref, wd_ref, gpost_ref, o_ref, h_sc, a_sc, *, n_f, tn):
    s = pl.program_id(1)
    cur = s % 2

    def up(slot):
        a = jnp.dot(h_sc[...], wu_ref[...], preferred_element_type=F32)
        a_sc[slot] = jnp.square(jnp.maximum(a, 0.0)).astype(BF16)

    def down(slot):
        a = a_sc[slot]
        for k in range(o_ref.shape[1] // tn):
            cs = slice(k * tn, (k + 1) * tn)
            o_ref[:, cs] += jnp.dot(a, wd_ref[:, cs], preferred_element_type=F32)

    @pl.when(s == 0)
    def _():
        h_sc[...] = _rms_scale(x_ref[...], gpre_ref[...]).astype(BF16)
        o_ref[...] = jnp.zeros(o_ref.shape, F32)
        up(cur)

    @pl.when(jnp.logical_and(s > 0, s < n_f))
    def _():
        down(1 - cur)
        up(cur)

    @pl.when(s == n_f)
    def _():
        down(1 - cur)
        o_ref[...] = x_ref[...] + _rms_scale(o_ref[...], gpost_ref[...])


def _mlp(x1, g_pre, wu, wd, g_post, *, tm=512, tf=512, tn=1024):
    t, d = x1.shape
    d_ff = wu.shape[1]
    n_f = d_ff // tf
    vmem = (2 * tm * d * 4 + 2 * tm * d * 4 + tm * d * 2 + 2 * 2 * d * tf * 2 + 2 * tm * tf * 2
            + 3 * tm * tf * 4 + 2 * tm * tn * 4 + tm * d * 4)
    return pl.pallas_call(
        functools.partial(_mlp_kernel, n_f=n_f, tn=tn),
        out_shape=jax.ShapeDtypeStruct((t, d), F32),
        grid=(t // tm, n_f + 1),
        in_specs=[
            pl.BlockSpec((tm, d), lambda i, s: (i, 0)),
            pl.BlockSpec((1, d), lambda i, s: (0, 0)),
            pl.BlockSpec((d, tf), lambda i, s: (0, jnp.minimum(s, n_f - 1))),
            pl.BlockSpec((tf, d), lambda i, s: (jnp.maximum(s - 1, 0), 0)),
            pl.BlockSpec((1, d), lambda i, s: (0, 0)),
        ],
        out_specs=pl.BlockSpec((tm, d), lambda i, s: (i, 0)),
        scratch_shapes=[pltpu.VMEM((tm, d), BF16), pltpu.VMEM((2, tm, tf), BF16)],
        compiler_params=pltpu.CompilerParams(
            dimension_semantics=("arbitrary", "arbitrary"), vmem_limit_bytes=vmem),
        name="mlp",
    )(x1, g_pre, wu, wd, g_post)


def kernel(x, norm_mix_pre, w_in, v_norm_g, v_norm_b, w_spatial, b_spatial, w_proj_a, lam_re, lam_im,
           log_dt, b_re, b_im, c_re, c_im, d_skip, w_glu_a, w_glu_b, w_out, norm_mix_post, norm_mlp_pre,
           w_ff_up, w_ff_down, norm_mlp_post):
    n_batch, seq, d_model = x.shape
    depth = w_in.shape[0]
    a_width = w_proj_a.shape[1]
    b_width = w_glu_a.shape[1]
    n_heads = w_spatial.shape[1]
    head_dim = a_width // n_heads
    x2 = x.reshape(n_batch * seq, d_model)
    for l in range(depth):
        proj, x_b, (wu, wo, wpa, wga, wgb) = _in_proj(
            x2, norm_mix_pre[l][None], w_in[l].astype(BF16),
            (w_ff_up[l], w_out[l], w_proj_a[l], w_glu_a[l], w_glu_b[l]),
            gelu_width=2 * a_width, lin_width=b_width)
        tables = _s5_tables(lam_re[l], lam_im[l], log_dt[l], b_re[l], b_im[l], c_re[l], c_im[l], d_skip[l])
        z_b = _s5_branch(x_b, tables, n_batch=n_batch, seq=seq)
        b_sp_full = jnp.repeat(b_spatial[l].T, head_dim, axis=1)
        mix_in, wd = _branches(proj, z_b, v_norm_g[l][None], v_norm_b[l][None], w_spatial[l].astype(BF16),
                               b_sp_full, wpa, wga, wgb, w_ff_down[l],
                               a_width=a_width, b_width=b_width, d_model=d_model)
        x1 = _out_proj(mix_in, wo, x2, norm_mix_post[l][None])
        x2 = _mlp(x1, norm_mlp_pre[l][None], wu, wd, norm_mlp_post[l][None])
    return x2.reshape(n_batch, seq, d_model)
```

```python
import functools

import jax
import jax.numpy as jnp
from jax import lax
from jax.experimental import pallas as pl
from jax.experimental.pallas import tpu as pltpu

F32 = jnp.float32
BF16 = jnp.bfloat16
NORM_EPS = 1e-6
HIGHEST = lax.Precision.HIGHEST

LANES = 128
A_CHUNK = 128
S5_CHUNK = 16
S5_PAIR = 2
S5_ROWS = 8


def _rms_scale(x, gain):
    ms = jnp.mean(x * x, axis=-1, keepdims=True)
    return x * lax.rsqrt(ms + NORM_EPS) * gain


def _serpentine(i, j, n):
    return jnp.where(i % 2 == 0, j, n - 1 - j)


def _sigmoid(x):
    return 0.5 * jnp.tanh(0.5 * x) + 0.5


def _cast_rider(w, n_i, n_j):
    r, c = w.shape
    blk = (r // n_i, c // n_j)
    assert r % n_i == 0 and c % n_j == 0 and blk[0] % 16 == 0 and blk[1] % LANES == 0

    def index(i, j):
        return i, jnp.minimum(j, n_j - 1)

    return pl.BlockSpec(blk, index), pl.BlockSpec(blk, index), jax.ShapeDtypeStruct(w.shape, BF16)


def _rider_bytes(w, n_i, n_j):
    return 2 * (w.size // (n_i * n_j)) * (4 + 2)


def _in_proj_kernel(*refs, gelu_tiles, n_riders, n_tiles):
    x_ref, g_ref, w_ref = refs[:3]
    cast_in = refs[3:3 + n_riders]
    o_ref, xb_ref = refs[3 + n_riders:5 + n_riders]
    cast_out = refs[5 + n_riders:5 + 2 * n_riders]
    h_sc, lin_sc = refs[5 + 2 * n_riders:]
    step = pl.program_id(1)
    j = _serpentine(pl.program_id(0), step, n_tiles)

    @pl.when(step == 0)
    def _():
        h_sc[...] = _rms_scale(x_ref[...], g_ref[...]).astype(BF16)

    for src, dst in zip(cast_in, cast_out):
        dst[...] = src[...].astype(BF16)

    acc = jnp.dot(h_sc[...], w_ref[...], preferred_element_type=F32)

    @pl.when(j < gelu_tiles)
    def _():
        o_ref[...] = jax.nn.gelu(acc).astype(BF16)

    @pl.when(j == gelu_tiles)
    def _():
        o_ref[...] = acc.astype(BF16)
        lc, n_c = xb_ref.shape[0], xb_ref.shape[1]
        for k in range(lin_sc.shape[0]):
            cs = slice(k * LANES, (k + 1) * LANES)
            lin_sc[k] = acc[:, cs]
            for tau in range(lc):
                xb_ref[tau, :, cs] = lin_sc[k, pl.ds(tau, n_c, stride=lc), :]

    @pl.when(j > gelu_tiles)
    def _():
        o_ref[...] = _sigmoid(acc).astype(BF16)


def _in_proj(x2, gain, w_bf16, riders, *, gelu_width, lin_width, tm=512, tn=1024, rider_cols=8):
    t, d = x2.shape
    n = w_bf16.shape[1]
    n_tiles = n // tn
    assert gelu_width % tn == 0 and lin_width == tn and t % tm == 0 and n % tn == 0 and n_tiles >= rider_cols
    assert tm % (8 * S5_CHUNK) == 0
    n_i = t // tm
    rider_specs = [_cast_rider(w, n_i, rider_cols) for w in riders]
    vmem = (2 * tm * d * 4 + tm * d * 2 + 2 * d * tn * 2 + 2 * tm * tn * 2 + 3 * tm * tn * 4
            + 8 * tm * tn * 4 + sum(_rider_bytes(w, n_i, rider_cols) for w in riders))
    outs = pl.pallas_call(
        functools.partial(_in_proj_kernel, gelu_tiles=gelu_width // tn, n_riders=len(riders), n_tiles=n_tiles),
        out_shape=(jax.ShapeDtypeStruct((t, n), BF16),
                   jax.ShapeDtypeStruct((S5_CHUNK, t // S5_CHUNK, lin_width), F32),
                   *[s[2] for s in rider_specs]),
        grid=(n_i, n_tiles),
        in_specs=[
            pl.BlockSpec((tm, d), lambda i, j: (i, 0)),
            pl.BlockSpec((1, d), lambda i, j: (0, 0)),
            pl.BlockSpec((d, tn), lambda i, j: (0, _serpentine(i, j, n_tiles))),
            *[s[0] for s in rider_specs],
        ],
        out_specs=(pl.BlockSpec((tm, tn), lambda i, j: (i, _serpentine(i, j, n_tiles))),
                   pl.BlockSpec((S5_CHUNK, tm // S5_CHUNK, lin_width), lambda i, j: (0, i, 0)),
                   *[s[1] for s in rider_specs]),
        scratch_shapes=[pltpu.VMEM((tm, d), BF16), pltpu.VMEM((lin_width // LANES, tm, LANES), F32)],
        compiler_params=pltpu.CompilerParams(
            dimension_semantics=("arbitrary", "arbitrary"), vmem_limit_bytes=vmem),
        name="in_proj",
    )(x2, gain, w_bf16, *riders)
    return outs[0], outs[1], outs[2:]


def _cmul(a, b):
    return a[0] * b[0] - a[1] * b[1], a[0] * b[1] + a[1] * b[0]


def _s5_tables(lam_re, lam_im, log_dt, b_re, b_im, c_re, c_im, d_skip):
    g, p = lam_re.shape
    hg = b_re.shape[-1]
    lc = S5_CHUNK
    npair = g // S5_PAIR
    dt = jnp.exp(log_dt)[:, None]
    mag = jnp.exp(lam_re * dt)
    a_bar = (mag * jnp.cos(lam_im * dt), mag * jnp.sin(lam_im * dt))
    den = lam_re * lam_re + lam_im * lam_im
    gain = (((a_bar[0] - 1.0) * lam_re + a_bar[1] * lam_im) / den,
            (a_bar[1] * lam_re - (a_bar[0] - 1.0) * lam_im) / den)
    b_bar = _cmul((gain[0][..., None], gain[1][..., None]), (b_re, b_im))
    pows = [(jnp.ones_like(mag), jnp.zeros_like(mag))]
    for _ in range(lc):
        pows.append(_cmul(pows[-1], a_bar))
    apow = (jnp.stack([q[0] for q in pows]), jnp.stack([q[1] for q in pows]))
    lanes = S5_PAIR * p
    own = (jnp.arange(S5_PAIR)[:, None] == (jnp.arange(lanes) // p)[None, :]).astype(F32)

    def pair_lanes(v):
        return jnp.transpose(v.reshape(v.shape[0], npair, lanes), (1, 0, 2))

    ap = (pair_lanes(apow[0]), pair_lanes(apow[1]))
    n_blk = S5_PAIR * hg

    def block_rows(v):
        return jnp.tile(v, (1, lc, 1))

    def power_rows(v):
        return jnp.repeat(v, n_blk, axis=1)

    bbt = [(jnp.transpose(v.reshape(npair, S5_PAIR, p, hg), (0, 3, 1, 2)).reshape(npair, 1, hg, lanes)
            * own[None, :, None, :]).reshape(npair, n_blk, lanes) for v in b_bar]
    e = _cmul((power_rows(ap[0][:, lc - 1::-1]), power_rows(ap[1][:, lc - 1::-1])),
              (block_rows(bbt[0]), block_rows(bbt[1])))
    w1_e = jnp.concatenate(e, axis=-1)
    cm = [jnp.transpose(v.reshape(npair, S5_PAIR, hg, p), (0, 1, 3, 2))[:, :, :, None, :]
          * jnp.eye(S5_PAIR, dtype=F32)[None, :, None, :, None] for v in (c_re, -c_im)]
    cm = jnp.concatenate([v.reshape(npair, lanes, S5_PAIR * hg) for v in cm], axis=1)
    kr = jnp.einsum('pxl,plo->pxo', w1_e, cm, precision=HIGHEST)
    skip = d_skip.reshape(npair, n_blk)[:, :, None] * jnp.eye(n_blk, dtype=F32)[None]
    kr = jnp.concatenate([kr[:, :(lc - 1) * n_blk], kr[:, (lc - 1) * n_blk:] + skip], axis=1)
    kr = jnp.pad(kr, ((0, 0), (0, (lc - 1) * n_blk), (0, 0)))
    w1_e = w1_e.astype(BF16)
    ct = [(jnp.tile(v.reshape(npair, S5_PAIR, hg, p), (1, 1, 1, S5_PAIR)) * own[None, :, None, :]
           ).reshape(npair, n_blk, lanes) for v in (c_re, c_im)]
    cin = _cmul((block_rows(ct[0]), block_rows(ct[1])),
                (power_rows(ap[0][:, 1:lc + 1]), power_rows(ap[1][:, 1:lc + 1])))
    cin = jnp.concatenate([cin[0], -cin[1]], axis=-1)
    cin = jnp.swapaxes(cin.astype(BF16), 1, 2)
    a_lc = (apow[0][lc], apow[1][lc])
    cpows = [a_lc]
    for _ in range(S5_ROWS - 1):
        cpows.append(_cmul(cpows[-1], a_lc))
    a_tab = jnp.stack([jnp.stack([q[0] for q in cpows]), jnp.stack([q[1] for q in cpows])])
    a_tab = jnp.transpose(a_tab.reshape(2, S5_ROWS, npair, S5_PAIR * p), (2, 0, 1, 3))
    return kr, w1_e, cin, a_tab


def _s5_kernel(x_ref, kr_ref, w1e_ref, cin_ref, a_ref, z_ref, w1y_sc, y_sc, e_sc, sp_sc, *, n_chunks, n_pair,
               pair_ch):
    lc = S5_CHUNK

    @pl.when(pl.program_id(1) == 0)
    def _():
        rows = lc * pair_ch
        for pp in range(n_pair):
            w1y_sc[pp] = jnp.concatenate(
                [kr_ref[pp, (lc - 1 - t) * pair_ch:(lc - 1 - t) * pair_ch + rows, :] for t in range(lc)],
                axis=1).astype(BF16)

    for pp in range(n_pair):
        lanes = slice(pp * pair_ch, (pp + 1) * pair_ch)
        u = jnp.concatenate([x_ref[tau][:, lanes] for tau in range(lc)], axis=1).astype(BF16)
        y_sc[pp] = jnp.dot(u, w1y_sc[pp], preferred_element_type=F32)
        e_sc[pp] = jnp.dot(u, w1e_ref[pp], preferred_element_type=F32)

    rid = lax.broadcasted_iota(jnp.int32, (S5_ROWS, LANES), 0)

    def shift_rows(v, k, fill):
        return jnp.where(rid >= k, pltpu.roll(v, k, 0), fill)

    def body(i, carry):
        new = []
        r0 = pl.multiple_of(i * S5_ROWS, S5_ROWS)
        for pp in range(n_pair):
            c_re, c_im = carry[2 * pp], carry[2 * pp + 1]
            p_re, p_im = a_ref[pp, 0], a_ref[pp, 1]
            x_re = e_sc[pp, pl.ds(r0, S5_ROWS), 0:LANES]
            x_im = e_sc[pp, pl.ds(r0, S5_ROWS), LANES:2 * LANES]
            k = 1
            while k < S5_ROWS:
                k_re, k_im = p_re[k - 1:k, :], p_im[k - 1:k, :]
                sh_re, sh_im = shift_rows(x_re, k, 0.0), shift_rows(x_im, k, 0.0)
                x_re, x_im = x_re + k_re * sh_re - k_im * sh_im, x_im + k_re * sh_im + k_im * sh_re
                k *= 2
            s_re = x_re + p_re * c_re - p_im * c_im
            s_im = x_im + p_re * c_im + p_im * c_re
            sp_sc[pp, pl.ds(r0, S5_ROWS), 0:LANES] = shift_rows(s_re, 1, c_re)
            sp_sc[pp, pl.ds(r0, S5_ROWS), LANES:2 * LANES] = shift_rows(s_im, 1, c_im)
            new.append(s_re[S5_ROWS - 1:S5_ROWS, :])
            new.append(s_im[S5_ROWS - 1:S5_ROWS, :])
        return tuple(new)

    init = tuple(jnp.zeros((1, LANES), F32) for _ in range(2 * n_pair))
    lax.fori_loop(0, n_chunks // S5_ROWS, body, init)

    ys = [jax.nn.gelu(y_sc[pp] + jnp.dot(sp_sc[pp].astype(BF16), cin_ref[pp], preferred_element_type=F32))
          for pp in range(n_pair)]
    for t in range(lc):
        z_t = jnp.concatenate([y[:, t * pair_ch:(t + 1) * pair_ch] for y in ys], axis=1)
        z_ref[pl.ds(t, n_chunks, stride=lc), :] = z_t


def _s5_branch(x_b, tables, *, n_batch, seq):
    kr, w1_e, cin, a_tab = tables
    npair, kw, _ = w1_e.shape
    lc, n_rows, bw = x_b.shape
    t = lc * n_rows
    n_chunks = seq // lc
    pair_ch = kw // lc
    n_pair = LANES // pair_ch
    assert lc == S5_CHUNK and n_chunks % S5_ROWS == 0 and bw % LANES == 0 and n_rows == n_batch * n_chunks
    assert kr.shape == (npair, (2 * lc - 1) * pair_ch, pair_ch)
    return pl.pallas_call(
        functools.partial(_s5_kernel, n_chunks=n_chunks, n_pair=n_pair, pair_ch=pair_ch),
        out_shape=jax.ShapeDtypeStruct((t, bw), F32),
        grid=(bw // LANES, n_batch),
        in_specs=[
            pl.BlockSpec((lc, n_chunks, LANES), lambda c, b: (0, b, c)),
            pl.BlockSpec((n_pair, (2 * lc - 1) * pair_ch, pair_ch), lambda c, b: (c, 0, 0)),
            pl.BlockSpec((n_pair, kw, 2 * LANES), lambda c, b: (c, 0, 0)),
            pl.BlockSpec((n_pair, 2 * LANES, kw), lambda c, b: (c, 0, 0)),
            pl.BlockSpec((n_pair, 2, S5_ROWS, LANES), lambda c, b: (c, 0, 0, 0)),
        ],
        out_specs=pl.BlockSpec((seq, LANES), lambda c, b: (b, c)),
        scratch_shapes=[pltpu.VMEM((n_pair, kw, kw), BF16),
                        pltpu.VMEM((n_pair, n_chunks, kw), F32),
                        pltpu.VMEM((n_pair, n_chunks, 2 * LANES), F32),
                        pltpu.VMEM((n_pair, n_chunks, 2 * LANES), F32)],
        compiler_params=pltpu.CompilerParams(dimension_semantics=("arbitrary", "arbitrary")),
        name="s5",
    )(x_b, kr, w1_e, cin, a_tab)


def _branches_kernel(u_ref, v_ref, ga_ref, gb_ref, zb_ref, lng_ref, lnb_ref, wsp_ref, bsp_ref,
                     wpa_ref, wga_ref, wgb_ref, cast_in_ref, o_ref, cast_out_ref, s_sc, *, n_heads, head_dim):
    j = pl.program_id(1)
    cast_out_ref[...] = cast_in_ref[...].astype(BF16)

    def glu_branch():
        zb = zb_ref[...].astype(BF16)
        glu_a = jnp.dot(zb, wga_ref[...], preferred_element_type=F32)
        glu_b = jnp.dot(zb, wgb_ref[...], preferred_element_type=F32)
        return gb_ref[...].astype(F32) * (glu_a * _sigmoid(glu_b))

    def merge(mix_b):
        br_a = jnp.dot(s_sc[...], wpa_ref[...], preferred_element_type=F32)
        o_ref[...] = (ga_ref[...].astype(F32) * br_a + mix_b).astype(BF16)

    def spatial_gating():
        v = v_ref[...].astype(F32)
        mu = jnp.mean(v, axis=-1, keepdims=True)
        vc = v - mu
        var = jnp.mean(vc * vc, axis=-1, keepdims=True)
        vn = (vc * lax.rsqrt(var + NORM_EPS) * lng_ref[...] + lnb_ref[...]).astype(BF16)
        n_c = v.shape[0] // A_CHUNK
        row = lax.broadcasted_iota(jnp.int32, (A_CHUNK, A_CHUNK), 0)
        col = lax.broadcasted_iota(jnp.int32, (A_CHUNK, A_CHUNK), 1)
        causal = row >= col
        for h in range(n_heads):
            hs = slice(h * head_dim, (h + 1) * head_dim)
            w = jnp.where(causal, wsp_ref[h], jnp.zeros((), BF16))
            rhs = jnp.concatenate(
                [vn[c * A_CHUNK:(c + 1) * A_CHUNK, hs] for c in range(n_c)], axis=1)
            mixed = jnp.dot(w, rhs, preferred_element_type=F32)
            bias = bsp_ref[:, hs]
            for c in range(n_c):
                rs = slice(c * A_CHUNK, (c + 1) * A_CHUNK)
                u_blk = u_ref[rs, hs].astype(F32)
                s_sc[rs, hs] = (u_blk * (mixed[:, c * head_dim:(c + 1) * head_dim] + bias)).astype(BF16)

    @pl.when(j == 0)
    def _():
        mix_b = glu_branch()
        spatial_gating()
        merge(mix_b)

    @pl.when(j > 0)
    def _():
        merge(glu_branch())


def _branches(proj, z_b, ln_g, ln_b, w_sp, b_sp_full, wpa, wga, wgb, rider, *, a_width, b_width, d_model,
              tm=512, tn=1024):
    t = proj.shape[0]
    n_heads = w_sp.shape[0]
    head_dim = a_width // n_heads
    ga_off = (2 * a_width + b_width) // tn
    gb_off = (2 * a_width + b_width + d_model) // tn
    assert (2 * a_width + b_width) % tn == 0 and d_model % tn == 0 and t % tm == 0 and tm % A_CHUNK == 0
    n_i, n_j = t // tm, d_model // tn
    rider_in, rider_out, rider_shape = _cast_rider(rider, n_i, n_j)
    vmem = (2 * 2 * tm * a_width * 2 + 2 * 2 * tm * tn * 2 + 2 * tm * b_width * 4
            + 2 * a_width * tn * 2 + 2 * 2 * b_width * tn * 2 + tm * a_width * 2 + 2 * tm * tn * 2
            + 4 * tm * a_width * 4 + 6 * tm * tn * 4 + _rider_bytes(rider, n_i, n_j))
    return pl.pallas_call(
        functools.partial(_branches_kernel, n_heads=n_heads, head_dim=head_dim),
        out_shape=(jax.ShapeDtypeStruct((t, d_model), BF16), rider_shape),
        grid=(n_i, n_j),
        in_specs=[
            pl.BlockSpec((tm, a_width), lambda i, j: (i, 0)),
            pl.BlockSpec((tm, a_width), lambda i, j: (i, 1)),
            pl.BlockSpec((tm, tn), lambda i, j: (i, ga_off + _serpentine(i, j, n_j))),
            pl.BlockSpec((tm, tn), lambda i, j: (i, gb_off + _serpentine(i, j, n_j))),
            pl.BlockSpec((tm, b_width), lambda i, j: (i, 0)),
            pl.BlockSpec((1, a_width), lambda i, j: (0, 0)),
            pl.BlockSpec((1, a_width), lambda i, j: (0, 0)),
            pl.BlockSpec((n_heads, A_CHUNK, A_CHUNK), lambda i, j: (0, 0, 0)),
            pl.BlockSpec((A_CHUNK, a_width), lambda i, j: (0, 0)),
            pl.BlockSpec((a_width, tn), lambda i, j: (0, _serpentine(i, j, n_j))),
            pl.BlockSpec((b_width, tn), lambda i, j: (0, _serpentine(i, j, n_j))),
            pl.BlockSpec((b_width, tn), lambda i, j: (0, _serpentine(i, j, n_j))),
            rider_in,
        ],
        out_specs=(pl.BlockSpec((tm, tn), lambda i, j: (i, _serpentine(i, j, n_j))), rider_out),
        scratch_shapes=[pltpu.VMEM((tm, a_width), BF16)],
        compiler_params=pltpu.CompilerParams(
            dimension_semantics=("arbitrary", "arbitrary"), vmem_limit_bytes=vmem),
        name="branches",
    )(proj, proj, proj, proj, z_b, ln_g, ln_b, w_sp, b_sp_full, wpa, wga, wgb, rider)


def _out_proj_kernel(m_ref, w_ref, x_ref, g_ref, o_ref, x_sc, *, n_tiles, tn):
    step = pl.program_id(1)
    j = _serpentine(pl.program_id(0), step, n_tiles)
    for k in range(n_tiles):
        @pl.when(j == k)
        def _(k=k):
            cs = slice(k * tn, (k + 1) * tn)
            o_ref[:, cs] = jnp.dot(m_ref[...], w_ref[...], preferred_element_type=F32)
            x_sc[:, cs] = x_ref[...]

    @pl.when(step == n_tiles - 1)
    def _():
        ssq = jnp.zeros((o_ref.shape[0], 1), F32)
        for k in range(n_tiles):
            a = o_ref[:, k * tn:(k + 1) * tn]
            ssq = ssq + jnp.sum(a * a, axis=-1, keepdims=True)
        inv = lax.rsqrt(ssq / (n_tiles * tn) + NORM_EPS)
        for k in range(n_tiles):
            cs = slice(k * tn, (k + 1) * tn)
            o_ref[:, cs] = x_sc[:, cs] + o_ref[:, cs] * inv * g_ref[:, cs]


def _out_proj(mix_in, w_bf16, x2, gain, *, tm=512, tn=1024):
    t, d = x2.shape
    n_tiles = d // tn
    vmem = (2 * tm * d * 2 + 2 * d * tn * 2 + 2 * tm * tn * 4 + 2 * tm * d * 4 + tm * d * 4 + 4 * tm * tn * 4)
    return pl.pallas_call(
        functools.partial(_out_proj_kernel, n_tiles=n_tiles, tn=tn),
        out_shape=jax.ShapeDtypeStruct((t, d), F32),
        grid=(t // tm, n_tiles),
        in_specs=[
            pl.BlockSpec((tm, d), lambda i, j: (i, 0)),
            pl.BlockSpec((d, tn), lambda i, j: (0, _serpentine(i, j, n_tiles))),
            pl.BlockSpec((tm, tn), lambda i, j: (i, _serpentine(i, j, n_tiles))),
            pl.BlockSpec((1, d), lambda i, j: (0, 0)),
        ],
        out_specs=pl.BlockSpec((tm, d), lambda i, j: (i, 0)),
        scratch_shapes=[pltpu.VMEM((tm, d), F32)],
        compiler_params=pltpu.CompilerParams(
            dimension_semantics=("arbitrary", "arbitrary"), vmem_limit_bytes=vmem),
        name="out_proj",
    )(mix_in, w_bf16, x2, gain)


def _mlp_kernel(x_ref, gpre_ref, wu_ref, wd_ref, gpost_ref, o_ref, h_sc, a_sc, *, n_f, tn):
    s = pl.program_id(1)
    cur = s % 2

    def up(slot):
        a = jnp.dot(h_sc[...], wu_ref[...], preferred_element_type=F32)
        a_sc[slot] = jnp.square(jnp.maximum(a, 0.0)).astype(BF16)

    def down(slot):
        a = a_sc[slot]
        for k in range(o_ref.shape[1] // tn):
            cs = slice(k * tn, (k + 1) * tn)
            o_ref[:, cs] += jnp.dot(a, wd_ref[:, cs], preferred_element_type=F32)

    @pl.when(s == 0)
    def _():
        h_sc[...] = _rms_scale(x_ref[...], gpre_ref[...]).astype(BF16)
        o_ref[...] = jnp.zeros(o_ref.shape, F32)
        up(cur)

    @pl.when(jnp.logical_and(s > 0, s < n_f))
    def _():
        down(1 - cur)
        up(cur)

    @pl.when(s == n_f)
    def _():
        down(1 - cur)
        o_ref[...] = x_ref[...] + _rms_scale(o_ref[...], gpost_ref[...])


def _mlp(x1, g_pre, wu, wd, g_post, *, tm=512, tf=512, tn=1024):
    t, d = x1.shape
    d_ff = wu.shape[1]
    n_f = d_ff // tf
    vmem = (2 * tm * d * 4 + 2 * tm * d * 4 + tm * d * 2 + 2 * 2 * d * tf * 2 + 2 * tm * tf * 2
            + 3 * tm * tf * 4 + 2 * tm * tn * 4 + tm * d * 4)
    return pl.pallas_call(
        functools.partial(_mlp_kernel, n_f=n_f, tn=tn),
        out_shape=jax.ShapeDtypeStruct((t, d), F32),
        grid=(t // tm, n_f + 1),
        in_specs=[
            pl.BlockSpec((tm, d), lambda i, s: (i, 0)),
            pl.BlockSpec((1, d), lambda i, s: (0, 0)),
            pl.BlockSpec((d, tf), lambda i, s: (0, _serpentine(i, jnp.minimum(s, n_f - 1), n_f))),
            pl.BlockSpec((tf, d), lambda i, s: (_serpentine(i, jnp.maximum(s - 1, 0), n_f), 0)),
            pl.BlockSpec((1, d), lambda i, s: (0, 0)),
        ],
        out_specs=pl.BlockSpec((tm, d), lambda i, s: (i, 0)),
        scratch_shapes=[pltpu.VMEM((tm, d), BF16), pltpu.VMEM((2, tm, tf), BF16)],
        compiler_params=pltpu.CompilerParams(
            dimension_semantics=("arbitrary", "arbitrary"), vmem_limit_bytes=vmem),
        name="mlp",
    )(x1, g_pre, wu, wd, g_post)


def kernel(x, norm_mix_pre, w_in, v_norm_g, v_norm_b, w_spatial, b_spatial, w_proj_a, lam_re, lam_im,
           log_dt, b_re, b_im, c_re, c_im, d_skip, w_glu_a, w_glu_b, w_out, norm_mix_post, norm_mlp_pre,
           w_ff_up, w_ff_down, norm_mlp_post):
    n_batch, seq, d_model = x.shape
    depth = w_in.shape[0]
    a_width = w_proj_a.shape[1]
    b_width = w_glu_a.shape[1]
    n_heads = w_spatial.shape[1]
    head_dim = a_width // n_heads
    x2 = x.reshape(n_batch * seq, d_model)
    for l in range(depth):
        proj, x_b, (wu, wo, wpa, wga, wgb) = _in_proj(
            x2, norm_mix_pre[l][None], w_in[l].astype(BF16),
            (w_ff_up[l], w_out[l], w_proj_a[l], w_glu_a[l], w_glu_b[l]),
            gelu_width=2 * a_width, lin_width=b_width)
        tables = _s5_tables(lam_re[l], lam_im[l], log_dt[l], b_re[l], b_im[l], c_re[l], c_im[l], d_skip[l])
        z_b = _s5_branch(x_b, tables, n_batch=n_batch, seq=seq)
        b_sp_full = jnp.repeat(b_spatial[l].T, head_dim, axis=1)
        mix_in, wd = _branches(proj, z_b, v_norm_g[l][None], v_norm_b[l][None], w_spatial[l].astype(BF16),
                               b_sp_full, wpa, wga, wgb, w_ff_down[l],
                               a_width=a_width, b_width=b_width, d_model=d_model)
        x1 = _out_proj(mix_in, wo, x2, norm_mix_post[l][None])
        x2 = _mlp(x1, norm_mlp_pre[l][None], wu, wd, norm_mlp_post[l][None])
    return x2.reshape(n_batch, seq, d_model)
```

```python
import functools

import jax
import jax.numpy as jnp
from jax import lax
from jax.experimental import pallas as pl
from jax.experimental.pallas import tpu as pltpu

F32 = jnp.float32
BF16 = jnp.bfloat16
NORM_EPS = 1e-6
HIGHEST = lax.Precision.HIGHEST

LANES = 128
A_CHUNK = 128
S5_CHUNK = 16
S5_PAIR = 2
S5_ROWS = 8


def _rms_scale(x, gain):
    ms = jnp.mean(x * x, axis=-1, keepdims=True)
    return x * lax.rsqrt(ms + NORM_EPS) * gain


def _serpentine(i, j, n):
    return jnp.where(i % 2 == 0, j, n - 1 - j)


def _sigmoid(x):
    return 0.5 * jnp.tanh(0.5 * x) + 0.5


def _cast_rider(w, n_i, n_j):
    r, c = w.shape
    blk = (r // n_i, c // n_j)
    assert r % n_i == 0 and c % n_j == 0 and blk[0] % 16 == 0 and blk[1] % LANES == 0

    def index(i, j):
        return i, jnp.minimum(j, n_j - 1)

    return pl.BlockSpec(blk, index), pl.BlockSpec(blk, index), jax.ShapeDtypeStruct(w.shape, BF16)


def _rider_bytes(w, n_i, n_j):
    return 2 * (w.size // (n_i * n_j)) * (4 + 2)


def _in_proj_kernel(*refs, gelu_tiles, n_riders, n_tiles):
    x_ref, g_ref, w_ref = refs[:3]
    cast_in = refs[3:3 + n_riders]
    o_ref, xb_ref = refs[3 + n_riders:5 + n_riders]
    cast_out = refs[5 + n_riders:5 + 2 * n_riders]
    h_sc, lin_sc = refs[5 + 2 * n_riders:]
    step = pl.program_id(1)
    j = _serpentine(pl.program_id(0), step, n_tiles)

    @pl.when(step == 0)
    def _():
        h_sc[...] = _rms_scale(x_ref[...], g_ref[...]).astype(BF16)

    for src, dst in zip(cast_in, cast_out):
        dst[...] = src[...].astype(BF16)

    acc = jnp.dot(h_sc[...], w_ref[...], preferred_element_type=F32)

    @pl.when(j < gelu_tiles)
    def _():
        o_ref[...] = jax.nn.gelu(acc).astype(BF16)

    @pl.when(j == gelu_tiles)
    def _():
        o_ref[...] = acc.astype(BF16)
        lc, n_c = xb_ref.shape[0], xb_ref.shape[1]
        for k in range(lin_sc.shape[0]):
            cs = slice(k * LANES, (k + 1) * LANES)
            lin_sc[k] = acc[:, cs]
            for tau in range(lc):
                xb_ref[tau, :, cs] = lin_sc[k, pl.ds(tau, n_c, stride=lc), :]

    @pl.when(j > gelu_tiles)
    def _():
        o_ref[...] = _sigmoid(acc).astype(BF16)


def _in_proj(x2, gain, w_bf16, riders, *, gelu_width, lin_width, tm=512, tn=1024, rider_cols=8):
    t, d = x2.shape
    n = w_bf16.shape[1]
    n_tiles = n // tn
    assert gelu_width % tn == 0 and lin_width == tn and t % tm == 0 and n % tn == 0 and n_tiles >= rider_cols
    assert tm % (8 * S5_CHUNK) == 0
    n_i = t // tm
    rider_specs = [_cast_rider(w, n_i, rider_cols) for w in riders]
    vmem = (2 * tm * d * 4 + tm * d * 2 + 2 * d * tn * 2 + 2 * tm * tn * 2 + 3 * tm * tn * 4
            + 8 * tm * tn * 4 + sum(_rider_bytes(w, n_i, rider_cols) for w in riders))
    outs = pl.pallas_call(
        functools.partial(_in_proj_kernel, gelu_tiles=gelu_width // tn, n_riders=len(riders), n_tiles=n_tiles),
        out_shape=(jax.ShapeDtypeStruct((t, n), BF16),
                   jax.ShapeDtypeStruct((S5_CHUNK, t // S5_CHUNK, lin_width), F32),
                   *[s[2] for s in rider_specs]),
        grid=(n_i, n_tiles),
        in_specs=[
            pl.BlockSpec((tm, d), lambda i, j: (i, 0)),
            pl.BlockSpec((1, d), lambda i, j: (0, 0)),
            pl.BlockSpec((d, tn), lambda i, j: (0, _serpentine(i, j, n_tiles))),
            *[s[0] for s in rider_specs],
        ],
        out_specs=(pl.BlockSpec((tm, tn), lambda i, j: (i, _serpentine(i, j, n_tiles))),
                   pl.BlockSpec((S5_CHUNK, tm // S5_CHUNK, lin_width), lambda i, j: (0, i, 0)),
                   *[s[1] for s in rider_specs]),
        scratch_shapes=[pltpu.VMEM((tm, d), BF16), pltpu.VMEM((lin_width // LANES, tm, LANES), F32)],
        compiler_params=pltpu.CompilerParams(
            dimension_semantics=("arbitrary", "arbitrary"), vmem_limit_bytes=vmem),
        name="in_proj",
    )(x2, gain, w_bf16, *riders)
    return outs[0], outs[1], outs[2:]


def _cmul(a, b):
    return a[0] * b[0] - a[1] * b[1], a[0] * b[1] + a[1] * b[0]


def _s5_tables(lam_re, lam_im, log_dt, b_re, b_im, c_re, c_im, d_skip):
    g, p = lam_re.shape
    hg = b_re.shape[-1]
    lc = S5_CHUNK
    npair = g // S5_PAIR
    dt = jnp.exp(log_dt)[:, None]
    mag = jnp.exp(lam_re * dt)
    a_bar = (mag * jnp.cos(lam_im * dt), mag * jnp.sin(lam_im * dt))
    den = lam_re * lam_re + lam_im * lam_im
    gain = (((a_bar[0] - 1.0) * lam_re + a_bar[1] * lam_im) / den,
            (a_bar[1] * lam_re - (a_bar[0] - 1.0) * lam_im) / den)
    b_bar = _cmul((gain[0][..., None], gain[1][..., None]), (b_re, b_im))
    pows = [(jnp.ones_like(mag), jnp.zeros_like(mag))]
    for _ in range(lc):
        pows.append(_cmul(pows[-1], a_bar))
    apow = (jnp.stack([q[0] for q in pows]), jnp.stack([q[1] for q in pows]))
    lanes = S5_PAIR * p
    own = (jnp.arange(S5_PAIR)[:, None] == (jnp.arange(lanes) // p)[None, :]).astype(F32)

    def pair_lanes(v):
        return jnp.transpose(v.reshape(v.shape[0], npair, lanes), (1, 0, 2))

    ap = (pair_lanes(apow[0]), pair_lanes(apow[1]))
    n_blk = S5_PAIR * hg

    def block_rows(v):
        return jnp.tile(v, (1, lc, 1))

    def power_rows(v):
        return jnp.repeat(v, n_blk, axis=1)

    bbt = [(jnp.transpose(v.reshape(npair, S5_PAIR, p, hg), (0, 3, 1, 2)).reshape(npair, 1, hg, lanes)
            * own[None, :, None, :]).reshape(npair, n_blk, lanes) for v in b_bar]
    e = _cmul((power_rows(ap[0][:, lc - 1::-1]), power_rows(ap[1][:, lc - 1::-1])),
              (block_rows(bbt[0]), block_rows(bbt[1])))
    w1_e = jnp.concatenate(e, axis=-1)
    cm = [jnp.transpose(v.reshape(npair, S5_PAIR, hg, p), (0, 1, 3, 2))[:, :, :, None, :]
          * jnp.eye(S5_PAIR, dtype=F32)[None, :, None, :, None] for v in (c_re, -c_im)]
    cm = jnp.concatenate([v.reshape(npair, lanes, S5_PAIR * hg) for v in cm], axis=1)
    kr = jnp.einsum('pxl,plo->pxo', w1_e, cm, precision=HIGHEST)
    skip = d_skip.reshape(npair, n_blk)[:, :, None] * jnp.eye(n_blk, dtype=F32)[None]
    kr = jnp.concatenate([kr[:, :(lc - 1) * n_blk], kr[:, (lc - 1) * n_blk:] + skip], axis=1)
    kr = jnp.pad(kr, ((0, 0), (0, (lc - 1) * n_blk), (0, 0)))
    w1_e = w1_e.astype(BF16)
    ct = [(jnp.tile(v.reshape(npair, S5_PAIR, hg, p), (1, 1, 1, S5_PAIR)) * own[None, :, None, :]
           ).reshape(npair, n_blk, lanes) for v in (c_re, c_im)]
    cin = _cmul((block_rows(ct[0]), block_rows(ct[1])),
                (power_rows(ap[0][:, 1:lc + 1]), power_rows(ap[1][:, 1:lc + 1])))
    cin = jnp.concatenate([cin[0], -cin[1]], axis=-1)
    cin = jnp.swapaxes(cin.astype(BF16), 1, 2)
    a_lc = (apow[0][lc], apow[1][lc])
    cpows = [a_lc]
    for _ in range(S5_ROWS - 1):
        cpows.append(_cmul(cpows[-1], a_lc))
    a_tab = jnp.stack([jnp.stack([q[0] for q in cpows]), jnp.stack([q[1] for q in cpows])])
    a_tab = jnp.transpose(a_tab.reshape(2, S5_ROWS, npair, S5_PAIR * p), (2, 0, 1, 3))
    return kr, w1_e, cin, a_tab


def _s5_kernel(x_ref, kr_ref, w1e_ref, cin_ref, a_ref, z_ref, w1y_sc, y_sc, e_sc, sp_sc, *, n_chunks, n_pair,
               pair_ch):
    lc = S5_CHUNK

    @pl.when(pl.program_id(1) == 0)
    def _():
        rows = lc * pair_ch
        for pp in range(n_pair):
            w1y_sc[pp] = jnp.concatenate(
                [kr_ref[pp, (lc - 1 - t) * pair_ch:(lc - 1 - t) * pair_ch + rows, :] for t in range(lc)],
                axis=1).astype(BF16)

    for pp in range(n_pair):
        lanes = slice(pp * pair_ch, (pp + 1) * pair_ch)
        u = jnp.concatenate([x_ref[tau][:, lanes] for tau in range(lc)], axis=1).astype(BF16)
        y_sc[pp] = jnp.dot(u, w1y_sc[pp], preferred_element_type=F32)
        e_sc[pp] = jnp.dot(u, w1e_ref[pp], preferred_element_type=F32)

    rid = lax.broadcasted_iota(jnp.int32, (S5_ROWS, LANES), 0)

    def shift_rows(v, k, fill):
        return jnp.where(rid >= k, pltpu.roll(v, k, 0), fill)

    def body(i, carry):
        new = []
        r0 = pl.multiple_of(i * S5_ROWS, S5_ROWS)
        for pp in range(n_pair):
            c_re, c_im = carry[2 * pp], carry[2 * pp + 1]
            p_re, p_im = a_ref[pp, 0], a_ref[pp, 1]
            x_re = e_sc[pp, pl.ds(r0, S5_ROWS), 0:LANES]
            x_im = e_sc[pp, pl.ds(r0, S5_ROWS), LANES:2 * LANES]
            k = 1
            while k < S5_ROWS:
                k_re, k_im = p_re[k - 1:k, :], p_im[k - 1:k, :]
                sh_re, sh_im = shift_rows(x_re, k, 0.0), shift_rows(x_im, k, 0.0)
                x_re, x_im = x_re + k_re * sh_re - k_im * sh_im, x_im + k_re * sh_im + k_im * sh_re
                k *= 2
            s_re = x_re + p_re * c_re - p_im * c_im
            s_im = x_im + p_re * c_im + p_im * c_re
            sp_sc[pp, pl.ds(r0, S5_ROWS), 0:LANES] = shift_rows(s_re, 1, c_re)
            sp_sc[pp, pl.ds(r0, S5_ROWS), LANES:2 * LANES] = shift_rows(s_im, 1, c_im)
            new.append(s_re[S5_ROWS - 1:S5_ROWS, :])
            new.append(s_im[S5_ROWS - 1:S5_ROWS, :])
        return tuple(new)

    init = tuple(jnp.zeros((1, LANES), F32) for _ in range(2 * n_pair))
    lax.fori_loop(0, n_chunks // S5_ROWS, body, init)

    ys = [jax.nn.gelu(y_sc[pp] + jnp.dot(sp_sc[pp].astype(BF16), cin_ref[pp], preferred_element_type=F32))
          for pp in range(n_pair)]
    for t in range(lc):
        z_t = jnp.concatenate([y[:, t * pair_ch:(t + 1) * pair_ch] for y in ys], axis=1)
        z_ref[pl.ds(t, n_chunks, stride=lc), :] = z_t


def _s5_branch(x_b, tables, *, n_batch, seq):
    kr, w1_e, cin, a_tab = tables
    npair, kw, _ = w1_e.shape
    lc, n_rows, bw = x_b.shape
    t = lc * n_rows
    n_chunks = seq // lc
    pair_ch = kw // lc
    n_pair = LANES // pair_ch
    assert lc == S5_CHUNK and n_chunks % S5_ROWS == 0 and bw % LANES == 0 and n_rows == n_batch * n_chunks
    assert kr.shape == (npair, (2 * lc - 1) * pair_ch, pair_ch)
    return pl.pallas_call(
        functools.partial(_s5_kernel, n_chunks=n_chunks, n_pair=n_pair, pair_ch=pair_ch),
        out_shape=jax.ShapeDtypeStruct((t, bw), F32),
        grid=(bw // LANES, n_batch),
        in_specs=[
            pl.BlockSpec((lc, n_chunks, LANES), lambda c, b: (0, b, c)),
            pl.BlockSpec((n_pair, (2 * lc - 1) * pair_ch, pair_ch), lambda c, b: (c, 0, 0)),
            pl.BlockSpec((n_pair, kw, 2 * LANES), lambda c, b: (c, 0, 0)),
            pl.BlockSpec((n_pair, 2 * LANES, kw), lambda c, b: (c, 0, 0)),
            pl.BlockSpec((n_pair, 2, S5_ROWS, LANES), lambda c, b: (c, 0, 0, 0)),
        ],
        out_specs=pl.BlockSpec((seq, LANES), lambda c, b: (b, c)),
        scratch_shapes=[pltpu.VMEM((n_pair, kw, kw), BF16),
                        pltpu.VMEM((n_pair, n_chunks, kw), F32),
                        pltpu.VMEM((n_pair, n_chunks, 2 * LANES), F32),
                        pltpu.VMEM((n_pair, n_chunks, 2 * LANES), F32)],
        compiler_params=pltpu.CompilerParams(dimension_semantics=("arbitrary", "arbitrary")),
        name="s5",
    )(x_b, kr, w1_e, cin, a_tab)


def _branches_kernel(u_ref, v_ref, ga_ref, gb_ref, zb_ref, lng_ref, lnb_ref, wsp_ref, bsp_ref,
                     wpa_ref, wga_ref, wgb_ref, cast_in_ref, o_ref, cast_out_ref, s_sc, *, n_heads, head_dim):
    j = pl.program_id(1)
    cast_out_ref[...] = cast_in_ref[...].astype(BF16)

    def glu_branch():
        zb = zb_ref[...].astype(BF16)
        glu_a = jnp.dot(zb, wga_ref[...], preferred_element_type=F32)
        glu_b = jnp.dot(zb, wgb_ref[...], preferred_element_type=F32)
        return gb_ref[...].astype(F32) * (glu_a * _sigmoid(glu_b))

    def merge(mix_b):
        br_a = jnp.dot(s_sc[...], wpa_ref[...], preferred_element_type=F32)
        o_ref[...] = (ga_ref[...].astype(F32) * br_a + mix_b).astype(BF16)

    def spatial_gating():
        v = v_ref[...].astype(F32)
        mu = jnp.mean(v, axis=-1, keepdims=True)
        vc = v - mu
        var = jnp.mean(vc * vc, axis=-1, keepdims=True)
        vn = (vc * lax.rsqrt(var + NORM_EPS) * lng_ref[...] + lnb_ref[...]).astype(BF16)
        n_c = v.shape[0] // A_CHUNK
        row = lax.broadcasted_iota(jnp.int32, (A_CHUNK, A_CHUNK), 0)
        col = lax.broadcasted_iota(jnp.int32, (A_CHUNK, A_CHUNK), 1)
        causal = row >= col
        for h in range(n_heads):
            hs = slice(h * head_dim, (h + 1) * head_dim)
            w = jnp.where(causal, wsp_ref[h], jnp.zeros((), BF16))
            rhs = jnp.concatenate(
                [vn[c * A_CHUNK:(c + 1) * A_CHUNK, hs] for c in range(n_c)], axis=1)
            mixed = jnp.dot(w, rhs, preferred_element_type=F32)
            bias = bsp_ref[:, hs]
            for c in range(n_c):
                rs = slice(c * A_CHUNK, (c + 1) * A_CHUNK)
                u_blk = u_ref[rs, hs].astype(F32)
                s_sc[rs, hs] = (u_blk * (mixed[:, c * head_dim:(c + 1) * head_dim] + bias)).astype(BF16)

    @pl.when(j == 0)
    def _():
        mix_b = glu_branch()
        spatial_gating()
        merge(mix_b)

    @pl.when(j > 0)
    def _():
        merge(glu_branch())


def _branches(proj, z_b, ln_g, ln_b, w_sp, b_sp_full, wpa, wga, wgb, rider, *, a_width, b_width, d_model,
              tm=512, tn=1024):
    t = proj.shape[0]
    n_heads = w_sp.shape[0]
    head_dim = a_width // n_heads
    ga_off = (2 * a_width + b_width) // tn
    gb_off = (2 * a_width + b_width + d_model) // tn
    assert (2 * a_width + b_width) % tn == 0 and d_model % tn == 0 and t % tm == 0 and tm % A_CHUNK == 0
    n_i, n_j = t // tm, d_model // tn
    rider_in, rider_out, rider_shape = _cast_rider(rider, n_i, n_j)
    vmem = (2 * 2 * tm * a_width * 2 + 2 * 2 * tm * tn * 2 + 2 * tm * b_width * 4
            + 2 * a_width * tn * 2 + 2 * 2 * b_width * tn * 2 + tm * a_width * 2 + 2 * tm * tn * 2
            + 4 * tm * a_width * 4 + 6 * tm * tn * 4 + _rider_bytes(rider, n_i, n_j))
    return pl.pallas_call(
        functools.partial(_branches_kernel, n_heads=n_heads, head_dim=head_dim),
        out_shape=(jax.ShapeDtypeStruct((t, d_model), BF16), rider_shape),
        grid=(n_i, n_j),
        in_specs=[
            pl.BlockSpec((tm, a_width), lambda i, j: (i, 0)),
            pl.BlockSpec((tm, a_width), lambda i, j: (i, 1)),
            pl.BlockSpec((tm, tn), lambda i, j: (i, ga_off + _serpentine(i, j, n_j))),
            pl.BlockSpec((tm, tn), lambda i, j: (i, gb_off + _serpentine(i, j, n_j))),
            pl.BlockSpec((tm, b_width), lambda i, j: (i, 0)),
            pl.BlockSpec((1, a_width), lambda i, j: (0, 0)),
            pl.BlockSpec((1, a_width), lambda i, j: (0, 0)),
            pl.BlockSpec((n_heads, A_CHUNK, A_CHUNK), lambda i, j: (0, 0, 0)),
            pl.BlockSpec((A_CHUNK, a_width), lambda i, j: (0, 0)),
            pl.BlockSpec((a_width, tn), lambda i, j: (0, _serpentine(i, j, n_j))),
            pl.BlockSpec((b_width, tn), lambda i, j: (0, _serpentine(i, j, n_j))),
            pl.BlockSpec((b_width, tn), lambda i, j: (0, _serpentine(i, j, n_j))),
            rider_in,
        ],
        out_specs=(pl.BlockSpec((tm, tn), lambda i, j: (i, _serpentine(i, j, n_j))), rider_out),
        scratch_shapes=[pltpu.VMEM((tm, a_width), BF16)],
        compiler_params=pltpu.CompilerParams(
            dimension_semantics=("arbitrary", "arbitrary"), vmem_limit_bytes=vmem),
        name="branches",
    )(proj, proj, proj, proj, z_b, ln_g, ln_b, w_sp, b_sp_full, wpa, wga, wgb, rider)


def _out_proj_kernel(m_ref, w_ref, x_ref, g_ref, o_ref, x_sc, *, n_tiles, tn):
    step = pl.program_id(1)
    j = _serpentine(pl.program_id(0), step, n_tiles)
    for k in range(n_tiles):
        @pl.when(j == k)
        def _(k=k):
            cs = slice(k * tn, (k + 1) * tn)
            o_ref[:, cs] = jnp.dot(m_ref[...], w_ref[...], preferred_element_type=F32)
            x_sc[:, cs] = x_ref[...]

    @pl.when(step == n_tiles - 1)
    def _():
        ssq = jnp.zeros((o_ref.shape[0], 1), F32)
        for k in range(n_tiles):
            a = o_ref[:, k * tn:(k + 1) * tn]
            ssq = ssq + jnp.sum(a * a, axis=-1, keepdims=True)
        inv = lax.rsqrt(ssq / (n_tiles * tn) + NORM_EPS)
        for k in range(n_tiles):
            cs = slice(k * tn, (k + 1) * tn)
            o_ref[:, cs] = x_sc[:, cs] + o_ref[:, cs] * inv * g_ref[:, cs]


def _out_proj(mix_in, w_bf16, x2, gain, *, tm=512, tn=1024):
    t, d = x2.shape
    n_tiles = d // tn
    vmem = (2 * tm * d * 2 + 2 * d * tn * 2 + 2 * tm * tn * 4 + 2 * tm * d * 4 + tm * d * 4 + 4 * tm * tn * 4)
    return pl.pallas_call(
        functools.partial(_out_proj_kernel, n_tiles=n_tiles, tn=tn),
        out_shape=jax.ShapeDtypeStruct((t, d), F32),
        grid=(t // tm, n_tiles),
        in_specs=[
            pl.BlockSpec((tm, d), lambda i, j: (i, 0)),
            pl.BlockSpec((d, tn), lambda i, j: (0, _serpentine(i, j, n_tiles))),
            pl.BlockSpec((tm, tn), lambda i, j: (i, _serpentine(i, j, n_tiles))),
            pl.BlockSpec((1, d), lambda i, j: (0, 0)),
        ],
        out_specs=pl.BlockSpec((tm, d), lambda i, j: (i, 0)),
        scratch_shapes=[pltpu.VMEM((tm, d), F32)],
        compiler_params=pltpu.CompilerParams(
            dimension_semantics=("arbitrary", "arbitrary"), vmem_limit_bytes=vmem),
        name="out_proj",
    )(mix_in, w_bf16, x2, gain)


def _mlp_kernel(x_hbm, gpre_ref, wu_ref, wd_ref, gpost_ref, o_ref, h_sc, a_sc, x_sc, x_sem, *, n_f, tn, n_rows):
    i = pl.program_id(0)
    s = pl.program_id(1)
    cur = s % 2
    row_slot = i % 2
    tm = x_sc.shape[1]

    def x_copy(row, slot):
        return pltpu.make_async_copy(x_hbm.at[pl.ds(pl.multiple_of(row * tm, tm), tm), :], x_sc.at[slot],
                                     x_sem.at[slot])

    @pl.when(jnp.logical_and(i == 0, s == 0))
    def _():
        x_copy(0, 0).start()
        x_copy(0, 0).wait()

    has_next = i + 1 < n_rows

    @pl.when(jnp.logical_and(s == 1, has_next))
    def _():
        x_copy(i + 1, 1 - row_slot).start()

    def up(slot):
        a = jnp.dot(h_sc[...], wu_ref[...], preferred_element_type=F32)
        a_sc[slot] = jnp.square(jnp.maximum(a, 0.0)).astype(BF16)

    def down(slot):
        a = a_sc[slot]
        for k in range(o_ref.shape[1] // tn):
            cs = slice(k * tn, (k + 1) * tn)
            o_ref[:, cs] += jnp.dot(a, wd_ref[:, cs], preferred_element_type=F32)

    @pl.when(s == 0)
    def _():
        h_sc[...] = _rms_scale(x_sc[row_slot], gpre_ref[...]).astype(BF16)
        o_ref[...] = jnp.zeros(o_ref.shape, F32)
        up(cur)

    @pl.when(jnp.logical_and(s > 0, s < n_f))
    def _():
        down(1 - cur)
        up(cur)

    @pl.when(s == n_f)
    def _():
        down(1 - cur)
        o_ref[...] = x_sc[row_slot] + _rms_scale(o_ref[...], gpost_ref[...])

    @pl.when(jnp.logical_and(s == n_f, has_next))
    def _():
        x_copy(i + 1, 1 - row_slot).wait()


def _mlp(x1, g_pre, wu, wd, g_post, *, tm=512, tf=512, tn=1024):
    t, d = x1.shape
    d_ff = wu.shape[1]
    n_f = d_ff // tf
    vmem = (2 * tm * d * 4 + 2 * tm * d * 4 + tm * d * 2 + 2 * 2 * d * tf * 2 + 2 * tm * tf * 2
            + 3 * tm * tf * 4 + 2 * tm * tn * 4 + tm * d * 4)
    return pl.pallas_call(
        functools.partial(_mlp_kernel, n_f=n_f, tn=tn, n_rows=t // tm),
        out_shape=jax.ShapeDtypeStruct((t, d), F32),
        grid=(t // tm, n_f + 1),
        in_specs=[
            pl.BlockSpec(memory_space=pl.ANY),
            pl.BlockSpec((1, d), lambda i, s: (0, 0)),
            pl.BlockSpec((d, tf), lambda i, s: (0, _serpentine(i, jnp.minimum(s, n_f - 1), n_f))),
            pl.BlockSpec((tf, d), lambda i, s: (_serpentine(i, jnp.maximum(s - 1, 0), n_f), 0)),
            pl.BlockSpec((1, d), lambda i, s: (0, 0)),
        ],
        out_specs=pl.BlockSpec((tm, d), lambda i, s: (i, 0)),
        scratch_shapes=[pltpu.VMEM((tm, d), BF16), pltpu.VMEM((2, tm, tf), BF16),
                        pltpu.VMEM((2, tm, d), F32), pltpu.SemaphoreType.DMA((2,))],
        compiler_params=pltpu.CompilerParams(
            dimension_semantics=("arbitrary", "arbitrary"), vmem_limit_bytes=vmem),
        name="mlp",
    )(x1, g_pre, wu, wd, g_post)


def kernel(x, norm_mix_pre, w_in, v_norm_g, v_norm_b, w_spatial, b_spatial, w_proj_a, lam_re, lam_im,
           log_dt, b_re, b_im, c_re, c_im, d_skip, w_glu_a, w_glu_b, w_out, norm_mix_post, norm_mlp_pre,
           w_ff_up, w_ff_down, norm_mlp_post):
    n_batch, seq, d_model = x.shape
    depth = w_in.shape[0]
    a_width = w_proj_a.shape[1]
    b_width = w_glu_a.shape[1]
    n_heads = w_spatial.shape[1]
    head_dim = a_width // n_heads
    x2 = x.reshape(n_batch * seq, d_model)
    for l in range(depth):
        proj, x_b, (wu, wo, wpa, wga, wgb) = _in_proj(
            x2, norm_mix_pre[l][None], w_in[l].astype(BF16),
            (w_ff_up[l], w_out[l], w_proj_a[l], w_glu_a[l], w_glu_b[l]),
            gelu_width=2 * a_width, lin_width=b_width)
        tables = _s5_tables(lam_re[l], lam_im[l], log_dt[l], b_re[l], b_im[l], c_re[l], c_im[l], d_skip[l])
        z_b = _s5_branch(x_b, tables, n_batch=n_batch, seq=seq)
        b_sp_full = jnp.repeat(b_spatial[l].T, head_dim, axis=1)
        mix_in, wd = _branches(proj, z_b, v_norm_g[l][None], v_norm_b[l][None], w_spatial[l].astype(BF16),
                               b_sp_full, wpa, wga, wgb, w_ff_down[l],
                               a_width=a_width, b_width=b_width, d_model=d_model)
        x1 = _out_proj(mix_in, wo, x2, norm_mix_post[l][None])
        x2 = _mlp(x1, norm_mlp_pre[l][None], wu, wd, norm_mlp_post[l][None])
    return x2.reshape(n_batch, seq, d_model)
```

```python
import functools

import jax
import jax.numpy as jnp
from jax import lax
from jax.experimental import pallas as pl
from jax.experimental.pallas import tpu as pltpu

F32 = jnp.float32
BF16 = jnp.bfloat16
NORM_EPS = 1e-6
HIGHEST = lax.Precision.HIGHEST

LANES = 128
A_CHUNK = 128
S5_CHUNK = 16
S5_PAIR = 2
S5_ROWS = 8


def _rms_scale(x, gain):
    ms = jnp.mean(x * x, axis=-1, keepdims=True)
    return x * lax.rsqrt(ms + NORM_EPS) * gain


def _serpentine(i, j, n):
    return jnp.where(i % 2 == 0, j, n - 1 - j)


def _sigmoid(x):
    return 0.5 * jnp.tanh(0.5 * x) + 0.5


def _cast_rider(w, n_i, n_j):
    r, c = w.shape
    blk = (r // n_i, c // n_j)
    assert r % n_i == 0 and c % n_j == 0 and blk[0] % 16 == 0 and blk[1] % LANES == 0

    def index(i, j):
        return i, jnp.minimum(j, n_j - 1)

    return pl.BlockSpec(blk, index), pl.BlockSpec(blk, index), jax.ShapeDtypeStruct(w.shape, BF16)


def _rider_bytes(w, n_i, n_j):
    return 2 * (w.size // (n_i * n_j)) * (4 + 2)


def _in_proj_kernel(*refs, gelu_tiles, n_riders, n_tiles):
    x_ref, g_ref, w_ref = refs[:3]
    cast_in = refs[3:3 + n_riders]
    o_ref, xb_ref = refs[3 + n_riders:5 + n_riders]
    cast_out = refs[5 + n_riders:5 + 2 * n_riders]
    h_sc, lin_sc = refs[5 + 2 * n_riders:]
    step = pl.program_id(1)
    j = _serpentine(pl.program_id(0), step, n_tiles)

    @pl.when(step == 0)
    def _():
        h_sc[...] = _rms_scale(x_ref[...], g_ref[...]).astype(BF16)

    for src, dst in zip(cast_in, cast_out):
        dst[...] = src[...].astype(BF16)

    acc = jnp.dot(h_sc[...], w_ref[...], preferred_element_type=F32)

    @pl.when(j < gelu_tiles)
    def _():
        o_ref[...] = jax.nn.gelu(acc).astype(BF16)

    @pl.when(j == gelu_tiles)
    def _():
        o_ref[...] = acc.astype(BF16)
        lc, n_c = xb_ref.shape[0], xb_ref.shape[1]
        for k in range(lin_sc.shape[0]):
            cs = slice(k * LANES, (k + 1) * LANES)
            lin_sc[k] = acc[:, cs]
            for tau in range(lc):
                xb_ref[tau, :, cs] = lin_sc[k, pl.ds(tau, n_c, stride=lc), :]

    @pl.when(j > gelu_tiles)
    def _():
        o_ref[...] = _sigmoid(acc).astype(BF16)


def _in_proj(x2, gain, w_bf16, riders, *, gelu_width, lin_width, tm=512, tn=1024, rider_cols=8):
    t, d = x2.shape
    n = w_bf16.shape[1]
    n_tiles = n // tn
    assert gelu_width % tn == 0 and lin_width == tn and t % tm == 0 and n % tn == 0 and n_tiles >= rider_cols
    assert tm % (8 * S5_CHUNK) == 0
    n_i = t // tm
    rider_specs = [_cast_rider(w, n_i, rider_cols) for w in riders]
    vmem = (2 * tm * d * 4 + tm * d * 2 + 2 * d * tn * 2 + 2 * tm * tn * 2 + 3 * tm * tn * 4
            + 8 * tm * tn * 4 + sum(_rider_bytes(w, n_i, rider_cols) for w in riders))
    outs = pl.pallas_call(
        functools.partial(_in_proj_kernel, gelu_tiles=gelu_width // tn, n_riders=len(riders), n_tiles=n_tiles),
        out_shape=(jax.ShapeDtypeStruct((t, n), BF16),
                   jax.ShapeDtypeStruct((S5_CHUNK, t // S5_CHUNK, lin_width), F32),
                   *[s[2] for s in rider_specs]),
        grid=(n_i, n_tiles),
        in_specs=[
            pl.BlockSpec((tm, d), lambda i, j: (i, 0)),
            pl.BlockSpec((1, d), lambda i, j: (0, 0)),
            pl.BlockSpec((d, tn), lambda i, j: (0, _serpentine(i, j, n_tiles))),
            *[s[0] for s in rider_specs],
        ],
        out_specs=(pl.BlockSpec((tm, tn), lambda i, j: (i, _serpentine(i, j, n_tiles))),
                   pl.BlockSpec((S5_CHUNK, tm // S5_CHUNK, lin_width), lambda i, j: (0, i, 0)),
                   *[s[1] for s in rider_specs]),
        scratch_shapes=[pltpu.VMEM((tm, d), BF16), pltpu.VMEM((lin_width // LANES, tm, LANES), F32)],
        compiler_params=pltpu.CompilerParams(
            dimension_semantics=("arbitrary", "arbitrary"), vmem_limit_bytes=vmem),
        name="in_proj",
    )(x2, gain, w_bf16, *riders)
    return outs[0], outs[1], outs[2:]


def _cmul(a, b):
    return a[0] * b[0] - a[1] * b[1], a[0] * b[1] + a[1] * b[0]


def _s5_tables(lam_re, lam_im, log_dt, b_re, b_im, c_re, c_im, d_skip):
    g, p = lam_re.shape
    hg = b_re.shape[-1]
    lc = S5_CHUNK
    npair = g // S5_PAIR
    dt = jnp.exp(log_dt)[:, None]
    mag = jnp.exp(lam_re * dt)
    a_bar = (mag * jnp.cos(lam_im * dt), mag * jnp.sin(lam_im * dt))
    den = lam_re * lam_re + lam_im * lam_im
    gain = (((a_bar[0] - 1.0) * lam_re + a_bar[1] * lam_im) / den,
            (a_bar[1] * lam_re - (a_bar[0] - 1.0) * lam_im) / den)
    b_bar = _cmul((gain[0][..., None], gain[1][..., None]), (b_re, b_im))
    pows = [(jnp.ones_like(mag), jnp.zeros_like(mag))]
    for _ in range(lc):
        pows.append(_cmul(pows[-1], a_bar))
    apow = (jnp.stack([q[0] for q in pows]), jnp.stack([q[1] for q in pows]))
    lanes = S5_PAIR * p
    own = (jnp.arange(S5_PAIR)[:, None] == (jnp.arange(lanes) // p)[None, :]).astype(F32)

    def pair_lanes(v):
        return jnp.transpose(v.reshape(v.shape[0], npair, lanes), (1, 0, 2))

    ap = (pair_lanes(apow[0]), pair_lanes(apow[1]))
    n_blk = S5_PAIR * hg

    def block_rows(v):
        return jnp.tile(v, (1, lc, 1))

    def power_rows(v):
        return jnp.repeat(v, n_blk, axis=1)

    bbt = [(jnp.transpose(v.reshape(npair, S5_PAIR, p, hg), (0, 3, 1, 2)).reshape(npair, 1, hg, lanes)
            * own[None, :, None, :]).reshape(npair, n_blk, lanes) for v in b_bar]
    e = _cmul((power_rows(ap[0][:, lc - 1::-1]), power_rows(ap[1][:, lc - 1::-1])),
              (block_rows(bbt[0]), block_rows(bbt[1])))
    w1_e = jnp.concatenate(e, axis=-1)
    cm = [jnp.transpose(v.reshape(npair, S5_PAIR, hg, p), (0, 1, 3, 2))[:, :, :, None, :]
          * jnp.eye(S5_PAIR, dtype=F32)[None, :, None, :, None] for v in (c_re, -c_im)]
    cm = jnp.concatenate([v.reshape(npair, lanes, S5_PAIR * hg) for v in cm], axis=1)
    kr = jnp.einsum('pxl,plo->pxo', w1_e, cm, precision=HIGHEST)
    skip = d_skip.reshape(npair, n_blk)[:, :, None] * jnp.eye(n_blk, dtype=F32)[None]
    kr = jnp.concatenate([kr[:, :(lc - 1) * n_blk], kr[:, (lc - 1) * n_blk:] + skip], axis=1)
    kr = jnp.pad(kr, ((0, 0), (0, (lc - 1) * n_blk), (0, 0)))
    w1_e = w1_e.astype(BF16)
    ct = [(jnp.tile(v.reshape(npair, S5_PAIR, hg, p), (1, 1, 1, S5_PAIR)) * own[None, :, None, :]
           ).reshape(npair, n_blk, lanes) for v in (c_re, c_im)]
    cin = _cmul((block_rows(ct[0]), block_rows(ct[1])),
                (power_rows(ap[0][:, 1:lc + 1]), power_rows(ap[1][:, 1:lc + 1])))
    cin = jnp.concatenate([cin[0], -cin[1]], axis=-1)
    cin = jnp.swapaxes(cin.astype(BF16), 1, 2)
    a_lc = (apow[0][lc], apow[1][lc])
    cpows = [a_lc]
    for _ in range(S5_ROWS - 1):
        cpows.append(_cmul(cpows[-1], a_lc))
    a_tab = jnp.stack([jnp.stack([q[0] for q in cpows]), jnp.stack([q[1] for q in cpows])])
    a_tab = jnp.transpose(a_tab.reshape(2, S5_ROWS, npair, S5_PAIR * p), (2, 0, 1, 3))
    return kr, w1_e, cin, a_tab


def _s5_kernel(x_ref, kr_ref, w1e_ref, cin_ref, a_ref, z_ref, w1y_sc, y_sc, e_sc, sp_sc, *, n_chunks, n_pair,
               pair_ch):
    lc = S5_CHUNK

    @pl.when(pl.program_id(1) == 0)
    def _():
        rows = lc * pair_ch
        for pp in range(n_pair):
            w1y_sc[pp] = jnp.concatenate(
                [kr_ref[pp, (lc - 1 - t) * pair_ch:(lc - 1 - t) * pair_ch + rows, :] for t in range(lc)],
                axis=1).astype(BF16)

    for pp in range(n_pair):
        lanes = slice(pp * pair_ch, (pp + 1) * pair_ch)
        u = jnp.concatenate([x_ref[tau][:, lanes] for tau in range(lc)], axis=1).astype(BF16)
        y_sc[pp] = jnp.dot(u, w1y_sc[pp], preferred_element_type=F32)
        e_sc[pp] = jnp.dot(u, w1e_ref[pp], preferred_element_type=F32)

    rid = lax.broadcasted_iota(jnp.int32, (S5_ROWS, LANES), 0)

    def shift_rows(v, k, fill):
        return jnp.where(rid >= k, pltpu.roll(v, k, 0), fill)

    def body(i, carry):
        new = []
        r0 = pl.multiple_of(i * S5_ROWS, S5_ROWS)
        for pp in range(n_pair):
            c_re, c_im = carry[2 * pp], carry[2 * pp + 1]
            p_re, p_im = a_ref[pp, 0], a_ref[pp, 1]
            x_re = e_sc[pp, pl.ds(r0, S5_ROWS), 0:LANES]
            x_im = e_sc[pp, pl.ds(r0, S5_ROWS), LANES:2 * LANES]
            k = 1
            while k < S5_ROWS:
                k_re, k_im = p_re[k - 1:k, :], p_im[k - 1:k, :]
                sh_re, sh_im = shift_rows(x_re, k, 0.0), shift_rows(x_im, k, 0.0)
                x_re, x_im = x_re + k_re * sh_re - k_im * sh_im, x_im + k_re * sh_im + k_im * sh_re
                k *= 2
            s_re = x_re + p_re * c_re - p_im * c_im
            s_im = x_im + p_re * c_im + p_im * c_re
            sp_sc[pp, pl.ds(r0, S5_ROWS), 0:LANES] = shift_rows(s_re, 1, c_re)
            sp_sc[pp, pl.ds(r0, S5_ROWS), LANES:2 * LANES] = shift_rows(s_im, 1, c_im)
            new.append(s_re[S5_ROWS - 1:S5_ROWS, :])
            new.append(s_im[S5_ROWS - 1:S5_ROWS, :])
        return tuple(new)

    init = tuple(jnp.zeros((1, LANES), F32) for _ in range(2 * n_pair))
    lax.fori_loop(0, n_chunks // S5_ROWS, body, init, unroll=2)

    ys = [jax.nn.gelu(y_sc[pp] + jnp.dot(sp_sc[pp].astype(BF16), cin_ref[pp], preferred_element_type=F32))
          for pp in range(n_pair)]
    for t in range(lc):
        z_t = jnp.concatenate([y[:, t * pair_ch:(t + 1) * pair_ch] for y in ys], axis=1)
        z_ref[pl.ds(t, n_chunks, stride=lc), :] = z_t


def _s5_branch(x_b, tables, *, n_batch, seq):
    kr, w1_e, cin, a_tab = tables
    npair, kw, _ = w1_e.shape
    lc, n_rows, bw = x_b.shape
    t = lc * n_rows
    n_chunks = seq // lc
    pair_ch = kw // lc
    n_pair = LANES // pair_ch
    assert lc == S5_CHUNK and n_chunks % S5_ROWS == 0 and bw % LANES == 0 and n_rows == n_batch * n_chunks
    assert kr.shape == (npair, (2 * lc - 1) * pair_ch, pair_ch)
    return pl.pallas_call(
        functools.partial(_s5_kernel, n_chunks=n_chunks, n_pair=n_pair, pair_ch=pair_ch),
        out_shape=jax.ShapeDtypeStruct((t, bw), F32),
        grid=(bw // LANES, n_batch),
        in_specs=[
            pl.BlockSpec((lc, n_chunks, LANES), lambda c, b: (0, b, c)),
            pl.BlockSpec((n_pair, (2 * lc - 1) * pair_ch, pair_ch), lambda c, b: (c, 0, 0)),
            pl.BlockSpec((n_pair, kw, 2 * LANES), lambda c, b: (c, 0, 0)),
            pl.BlockSpec((n_pair, 2 * LANES, kw), lambda c, b: (c, 0, 0)),
            pl.BlockSpec((n_pair, 2, S5_ROWS, LANES), lambda c, b: (c, 0, 0, 0)),
        ],
        out_specs=pl.BlockSpec((seq, LANES), lambda c, b: (b, c)),
        scratch_shapes=[pltpu.VMEM((n_pair, kw, kw), BF16),
                        pltpu.VMEM((n_pair, n_chunks, kw), F32),
                        pltpu.VMEM((n_pair, n_chunks, 2 * LANES), F32),
                        pltpu.VMEM((n_pair, n_chunks, 2 * LANES), F32)],
        compiler_params=pltpu.CompilerParams(dimension_semantics=("arbitrary", "arbitrary")),
        name="s5",
    )(x_b, kr, w1_e, cin, a_tab)


def _branches_kernel(u_ref, v_ref, ga_ref, gb_ref, zb_ref, lng_ref, lnb_ref, wsp_ref, bsp_ref,
                     wpa_ref, wga_ref, wgb_ref, cast_in_ref, o_ref, cast_out_ref, s_sc, *, n_heads, head_dim):
    j = pl.program_id(1)
    cast_out_ref[...] = cast_in_ref[...].astype(BF16)

    def glu_branch():
        zb = zb_ref[...].astype(BF16)
        glu_a = jnp.dot(zb, wga_ref[...], preferred_element_type=F32)
        glu_b = jnp.dot(zb, wgb_ref[...], preferred_element_type=F32)
        return gb_ref[...].astype(F32) * (glu_a * _sigmoid(glu_b))

    def merge(mix_b):
        br_a = jnp.dot(s_sc[...], wpa_ref[...], preferred_element_type=F32)
        o_ref[...] = (ga_ref[...].astype(F32) * br_a + mix_b).astype(BF16)

    def spatial_gating():
        v = v_ref[...].astype(F32)
        mu = jnp.mean(v, axis=-1, keepdims=True)
        vc = v - mu
        var = jnp.mean(vc * vc, axis=-1, keepdims=True)
        vn = (vc * lax.rsqrt(var + NORM_EPS) * lng_ref[...] + lnb_ref[...]).astype(BF16)
        n_c = v.shape[0] // A_CHUNK
        row = lax.broadcasted_iota(jnp.int32, (A_CHUNK, A_CHUNK), 0)
        col = lax.broadcasted_iota(jnp.int32, (A_CHUNK, A_CHUNK), 1)
        causal = row >= col
        for h in range(n_heads):
            hs = slice(h * head_dim, (h + 1) * head_dim)
            w = jnp.where(causal, wsp_ref[h], jnp.zeros((), BF16))
            rhs = jnp.concatenate(
                [vn[c * A_CHUNK:(c + 1) * A_CHUNK, hs] for c in range(n_c)], axis=1)
            mixed = jnp.dot(w, rhs, preferred_element_type=F32)
            bias = bsp_ref[:, hs]
            for c in range(n_c):
                rs = slice(c * A_CHUNK, (c + 1) * A_CHUNK)
                u_blk = u_ref[rs, hs].astype(F32)
                s_sc[rs, hs] = (u_blk * (mixed[:, c * head_dim:(c + 1) * head_dim] + bias)).astype(BF16)

    @pl.when(j == 0)
    def _():
        mix_b = glu_branch()
        spatial_gating()
        merge(mix_b)

    @pl.when(j > 0)
    def _():
        merge(glu_branch())


def _branches(proj, z_b, ln_g, ln_b, w_sp, b_sp_full, wpa, wga, wgb, rider, *, a_width, b_width, d_model,
              tm=512, tn=1024):
    t = proj.shape[0]
    n_heads = w_sp.shape[0]
    head_dim = a_width // n_heads
    ga_off = (2 * a_width + b_width) // tn
    gb_off = (2 * a_width + b_width + d_model) // tn
    assert (2 * a_width + b_width) % tn == 0 and d_model % tn == 0 and t % tm == 0 and tm % A_CHUNK == 0
    n_i, n_j = t // tm, d_model // tn
    rider_in, rider_out, rider_shape = _cast_rider(rider, n_i, n_j)
    vmem = (2 * 2 * tm * a_width * 2 + 2 * 2 * tm * tn * 2 + 2 * tm * b_width * 4
            + 2 * a_width * tn * 2 + 2 * 2 * b_width * tn * 2 + tm * a_width * 2 + 2 * tm * tn * 2
            + 4 * tm * a_width * 4 + 6 * tm * tn * 4 + _rider_bytes(rider, n_i, n_j))
    return pl.pallas_call(
        functools.partial(_branches_kernel, n_heads=n_heads, head_dim=head_dim),
        out_shape=(jax.ShapeDtypeStruct((t, d_model), BF16), rider_shape),
        grid=(n_i, n_j),
        in_specs=[
            pl.BlockSpec((tm, a_width), lambda i, j: (i, 0)),
            pl.BlockSpec((tm, a_width), lambda i, j: (i, 1)),
            pl.BlockSpec((tm, tn), lambda i, j: (i, ga_off + _serpentine(i, j, n_j))),
            pl.BlockSpec((tm, tn), lambda i, j: (i, gb_off + _serpentine(i, j, n_j))),
            pl.BlockSpec((tm, b_width), lambda i, j: (i, 0)),
            pl.BlockSpec((1, a_width), lambda i, j: (0, 0)),
            pl.BlockSpec((1, a_width), lambda i, j: (0, 0)),
            pl.BlockSpec((n_heads, A_CHUNK, A_CHUNK), lambda i, j: (0, 0, 0)),
            pl.BlockSpec((A_CHUNK, a_width), lambda i, j: (0, 0)),
            pl.BlockSpec((a_width, tn), lambda i, j: (0, _serpentine(i, j, n_j))),
            pl.BlockSpec((b_width, tn), lambda i, j: (0, _serpentine(i, j, n_j))),
            pl.BlockSpec((b_width, tn), lambda i, j: (0, _serpentine(i, j, n_j))),
            rider_in,
        ],
        out_specs=(pl.BlockSpec((tm, tn), lambda i, j: (i, _serpentine(i, j, n_j))), rider_out),
        scratch_shapes=[pltpu.VMEM((tm, a_width), BF16)],
        compiler_params=pltpu.CompilerParams(
            dimension_semantics=("arbitrary", "arbitrary"), vmem_limit_bytes=vmem),
        name="branches",
    )(proj, proj, proj, proj, z_b, ln_g, ln_b, w_sp, b_sp_full, wpa, wga, wgb, rider)


def _out_proj_kernel(m_ref, w_ref, x_ref, g_ref, o_ref, x_sc, *, n_tiles, tn):
    step = pl.program_id(1)
    j = _serpentine(pl.program_id(0), step, n_tiles)
    for k in range(n_tiles):
        @pl.when(j == k)
        def _(k=k):
            cs = slice(k * tn, (k + 1) * tn)
            o_ref[:, cs] = jnp.dot(m_ref[...], w_ref[...], preferred_element_type=F32)
            x_sc[:, cs] = x_ref[...]

    @pl.when(step == n_tiles - 1)
    def _():
        ssq = jnp.zeros((o_ref.shape[0], 1), F32)
        for k in range(n_tiles):
            a = o_ref[:, k * tn:(k + 1) * tn]
            ssq = ssq + jnp.sum(a * a, axis=-1, keepdims=True)
        inv = lax.rsqrt(ssq / (n_tiles * tn) + NORM_EPS)
        for k in range(n_tiles):
            cs = slice(k * tn, (k + 1) * tn)
            o_ref[:, cs] = x_sc[:, cs] + o_ref[:, cs] * inv * g_ref[:, cs]


def _out_proj(mix_in, w_bf16, x2, gain, *, tm=512, tn=1024):
    t, d = x2.shape
    n_tiles = d // tn
    vmem = (2 * tm * d * 2 + 2 * d * tn * 2 + 2 * tm * tn * 4 + 2 * tm * d * 4 + tm * d * 4 + 4 * tm * tn * 4)
    return pl.pallas_call(
        functools.partial(_out_proj_kernel, n_tiles=n_tiles, tn=tn),
        out_shape=jax.ShapeDtypeStruct((t, d), F32),
        grid=(t // tm, n_tiles),
        in_specs=[
            pl.BlockSpec((tm, d), lambda i, j: (i, 0)),
            pl.BlockSpec((d, tn), lambda i, j: (0, _serpentine(i, j, n_tiles))),
            pl.BlockSpec((tm, tn), lambda i, j: (i, _serpentine(i, j, n_tiles))),
            pl.BlockSpec((1, d), lambda i, j: (0, 0)),
        ],
        out_specs=pl.BlockSpec((tm, d), lambda i, j: (i, 0)),
        scratch_shapes=[pltpu.VMEM((tm, d), F32)],
        compiler_params=pltpu.CompilerParams(
            dimension_semantics=("arbitrary", "arbitrary"), vmem_limit_bytes=vmem),
        name="out_proj",
    )(mix_in, w_bf16, x2, gain)


def _mlp_kernel(x_ref, gpre_ref, wu_ref, wd_ref, gpost_ref, o_ref, h_sc, a_sc, *, n_f, tn):
    s = pl.program_id(1)
    cur = s % 2

    def up(slot):
        a = jnp.dot(h_sc[...], wu_ref[...], preferred_element_type=F32)
        a_sc[slot] = jnp.square(jnp.maximum(a, 0.0)).astype(BF16)

    def down(slot):
        a = a_sc[slot]
        for k in range(o_ref.shape[1] // tn):
            cs = slice(k * tn, (k + 1) * tn)
            o_ref[:, cs] += jnp.dot(a, wd_ref[:, cs], preferred_element_type=F32)

    @pl.when(s == 0)
    def _():
        h_sc[...] = _rms_scale(x_ref[...], gpre_ref[...]).astype(BF16)
        o_ref[...] = jnp.zeros(o_ref.shape, F32)
        up(cur)

    @pl.when(jnp.logical_and(s > 0, s < n_f))
    def _():
        down(1 - cur)
        up(cur)

    @pl.when(s == n_f)
    def _():
        down(1 - cur)
        o_ref[...] = x_ref[...] + _rms_scale(o_ref[...], gpost_ref[...])


def _mlp(x1, g_pre, wu, wd, g_post, *, tm=512, tf=512, tn=1024):
    t, d = x1.shape
    d_ff = wu.shape[1]
    n_f = d_ff // tf
    vmem = (2 * tm * d * 4 + 2 * tm * d * 4 + tm * d * 2 + 2 * 2 * d * tf * 2 + 2 * tm * tf * 2
            + 3 * tm * tf * 4 + 2 * tm * tn * 4 + tm * d * 4)
    return pl.pallas_call(
        functools.partial(_mlp_kernel, n_f=n_f, tn=tn),
        out_shape=jax.ShapeDtypeStruct((t, d), F32),
        grid=(t // tm, n_f + 1),
        in_specs=[
            pl.BlockSpec((tm, d), lambda i, s: (i, 0)),
            pl.BlockSpec((1, d), lambda i, s: (0, 0)),
            pl.BlockSpec((d, tf), lambda i, s: (0, _serpentine(i, jnp.minimum(s, n_f - 1), n_f))),
            pl.BlockSpec((tf, d), lambda i, s: (_serpentine(i, jnp.maximum(s - 1, 0), n_f), 0)),
            pl.BlockSpec((1, d), lambda i, s: (0, 0)),
        ],
        out_specs=pl.BlockSpec((tm, d), lambda i, s: (i, 0)),
        scratch_shapes=[pltpu.VMEM((tm, d), BF16), pltpu.VMEM((2, tm, tf), BF16)],
        compiler_params=pltpu.CompilerParams(
            dimension_semantics=("arbitrary", "arbitrary"), vmem_limit_bytes=vmem),
        name="mlp",
    )(x1, g_pre, wu, wd, g_post)


def kernel(x, norm_mix_pre, w_in, v_norm_g, v_norm_b, w_spatial, b_spatial, w_proj_a, lam_re, lam_im,
           log_dt, b_re, b_im, c_re, c_im, d_skip, w_glu_a, w_glu_b, w_out, norm_mix_post, norm_mlp_pre,
           w_ff_up, w_ff_down, norm_mlp_post):
    n_batch, seq, d_model = x.shape
    depth = w_in.shape[0]
    a_width = w_proj_a.shape[1]
    b_width = w_glu_a.shape[1]
    n_heads = w_spatial.shape[1]
    head_dim = a_width // n_heads
    x2 = x.reshape(n_batch * seq, d_model)
    for l in range(depth):
        proj, x_b, (wu, wo, wpa, wga, wgb) = _in_proj(
            x2, norm_mix_pre[l][None], w_in[l].astype(BF16),
            (w_ff_up[l], w_out[l], w_proj_a[l], w_glu_a[l], w_glu_b[l]),
            gelu_width=2 * a_width, lin_width=b_width)
        tables = _s5_tables(lam_re[l], lam_im[l], log_dt[l], b_re[l], b_im[l], c_re[l], c_im[l], d_skip[l])
        z_b = _s5_branch(x_b, tables, n_batch=n_batch, seq=seq)
        b_sp_full = jnp.repeat(b_spatial[l].T, head_dim, axis=1)
        mix_in, wd = _branches(proj, z_b, v_norm_g[l][None], v_norm_b[l][None], w_spatial[l].astype(BF16),
                               b_sp_full, wpa, wga, wgb, w_ff_down[l],
                               a_width=a_width, b_width=b_width, d_model=d_model)
        x1 = _out_proj(mix_in, wo, x2, norm_mix_post[l][None])
        x2 = _mlp(x1, norm_mlp_pre[l][None], wu, wd, norm_mlp_post[l][None])
    return x2.reshape(n_batch, seq, d_model)
```
